```python
import math
import jax, jax.numpy as jnp
from jax import lax
import numpy as np

D_MODEL = 1024
BATCH = 16
SEQ = 2048
DEPTH = 1

MEM_LEN = 256
EPS = 1e-6

MIX_WIDTH = D_MODEL
ML_WIDTH = MIX_WIDTH // 2
DA_WIDTH = MIX_WIDTH - ML_WIDTH
ML_HEADS = 4
ML_HEAD_DIM = ML_WIDTH // ML_HEADS
ML_CHUNK = 64
CONV_WIDTH = 4
DA_HEADS = 4
DA_VDIM = DA_WIDTH // DA_HEADS
DA_QK_DIM = DA_VDIM // 2
Q_BLOCK = 128

CA_HEADS = 4
CA_HEAD_DIM = D_MODEL // CA_HEADS

PEER_HEADS = 8
PEER_KEYS = 128
PEER_EXPERTS = PEER_KEYS * PEER_KEYS
PEER_TOPK = 16
PEER_DKEY = 128
PEER_HALF = PEER_DKEY // 2
PEER_BLOCK = 128

IN_SPLITS = (2 * ML_WIDTH,
             ML_WIDTH,
             ML_WIDTH,
             ML_HEADS,
             ML_HEADS,
             2 * DA_HEADS * DA_QK_DIM,
             2 * DA_HEADS * DA_QK_DIM,
             DA_WIDTH)
IN_WIDTH = sum(IN_SPLITS)
SPLIT_IDX = tuple(int(c) for c in np.cumsum(IN_SPLITS)[:-1])

kernel_name = "hymba_mlstm_diffattn_peer_layer"


def rmsnorm(x, g):
    xf = x.astype(jnp.float32)
    y = xf * lax.rsqrt(jnp.mean(xf * xf, axis=-1, keepdims=True) + EPS)
    return (y * g.astype(jnp.float32)).astype(x.dtype)


def alibi_slopes(n_heads):
    return jnp.asarray(np.array([2.0 ** (-8.0 * (i + 1) / n_heads) for i in range(n_heads)], np.float32))


def causal_depthwise_conv(x, w):
    c = x.shape[-1]
    return lax.conv_general_dilated(x, w[:, None, :].astype(x.dtype), window_strides=(1,),
                                    padding=[(CONV_WIDTH - 1, 0)],
                                    dimension_numbers=('NWC', 'WIO', 'NWC'),
                                    feature_group_count=c)


def mlstm_chunkwise(q, k, v, ig, lf):
    b_, s_, h_, d_ = q.shape
    nc = s_ // ML_CHUNK
    k = k * (d_ ** -0.5)

    def chunks4(a):
        return a.reshape(b_, nc, ML_CHUNK, h_, d_).transpose(1, 0, 3, 2, 4)

    def chunks3(a):
        return a.reshape(b_, nc, ML_CHUNK, h_).transpose(1, 0, 3, 2)

    causal = jnp.tril(jnp.ones((ML_CHUNK, ML_CHUNK), dtype=bool))

    def step(carry, inp):
        c_st, n_st, m_st = carry
        qc, kc, vc, ic, fc = inp
        bcum = jnp.cumsum(fc, axis=-1)
        log_d = bcum[..., :, None] - bcum[..., None, :] + ic[..., None, :]
        log_d = jnp.where(causal, log_d, -jnp.inf)
        m_inter = bcum + m_st[..., None]
        m_t = jnp.maximum(m_inter, jnp.max(log_d, axis=-1))
        dmat = jnp.exp(log_d - m_t[..., None])
        sc = jnp.einsum('bhld,bhsd->bhls', qc, kc) * dmat
        inter = jnp.exp(m_inter - m_t)
        num = jnp.einsum('bhls,bhse->bhle', sc, vc) + inter[..., None] * jnp.einsum('bhld,bhde->bhle', qc, c_st)
        den = jnp.sum(sc, axis=-1) + inter * jnp.einsum('bhld,bhd->bhl', qc, n_st)
        h = num / jnp.maximum(jnp.abs(den), jnp.exp(-m_t))[..., None]
        g = bcum[..., -1:] - bcum + ic
        m_next = jnp.maximum(bcum[..., -1] + m_st, jnp.max(g, axis=-1))
        decay = jnp.exp(bcum[..., -1] + m_st - m_next)
        wgt = jnp.exp(g - m_next[..., None])
        c_next = decay[..., None, None] * c_st + jnp.einsum('bhs,bhsd,bhse->bhde', wgt, kc, vc)
        n_next = decay[..., None] * n_st + jnp.einsum('bhs,bhsd->bhd', wgt, kc)
        return (c_next, n_next, m_next), h

    init = (jnp.zeros((b_, h_, d_, d_), jnp.float32),
            jnp.zeros((b_, h_, d_), jnp.float32),
            jnp.zeros((b_, h_), jnp.float32))
    _, hs = lax.scan(step, init, (chunks4(q), chunks4(k), chunks4(v), chunks3(ig), chunks3(lf)))
    return hs.transpose(1, 0, 3, 2, 4).reshape(b_, s_, h_, d_)


def diff_attention(q, k, v, lam, lam_init, norm_g):
    b_, s_, h_, _, d_ = q.shape
    nb = s_ // Q_BLOCK
    qb = q.reshape(b_, nb, Q_BLOCK, h_, 2, d_).transpose(1, 0, 2, 3, 4, 5)
    starts = jnp.arange(nb, dtype=jnp.int32) * Q_BLOCK
    kpos = jnp.arange(s_, dtype=jnp.int32)
    slopes = alibi_slopes(h_)
    scale = d_ ** -0.5

    def block(args):
        qblk, start = args
        qpos = start + jnp.arange(Q_BLOCK, dtype=jnp.int32)
        dist = (qpos[:, None] - kpos[None, :]).astype(jnp.float32)
        sc = jnp.einsum('bqhcd,bkhcd->bhcqk', qblk, k).astype(jnp.float32) * scale
        sc = sc - slopes[None, :, None, None, None] * dist
        sc = jnp.where(dist >= 0, sc, -jnp.inf)
        p = jax.nn.softmax(sc, axis=-1)
        a = p[:, :, 0] - lam * p[:, :, 1]
        return jnp.einsum('bhqk,bkhe->bqhe', a.astype(v.dtype), v)

    o = lax.map(block, (qb, starts))
    o = o.transpose(1, 0, 2, 3, 4).reshape(b_, s_, h_, 2 * d_)
    return rmsnorm(o, norm_g.reshape(h_, 2 * d_)) * (1.0 - lam_init)


def parallel_mixer(h, w_in, conv_w, b_igate, b_fgate, ml_norm_g, lambda_q1, lambda_k1,
                   lambda_q2, lambda_k2, da_norm_g, w_out, layer_idx):
    b_, s_, _ = h.shape
    f32 = jnp.float32
    z = h @ w_in
    ml_qk, ml_v, ml_o, ml_i, ml_f, da_q, da_k, da_v = jnp.split(z, SPLIT_IDX, axis=-1)

    ml_qk = jax.nn.silu(causal_depthwise_conv(ml_qk, conv_w))
    ml_q, ml_k = jnp.split(ml_qk, 2, axis=-1)
    shp = (b_, s_, ML_HEADS, ML_HEAD_DIM)
    ig = (ml_i + b_igate).astype(f32)
    lf = jax.nn.log_sigmoid((ml_f + b_fgate).astype(f32))
    h_ml = mlstm_chunkwise(ml_q.reshape(shp).astype(f32), ml_k.reshape(shp).astype(f32),
                           ml_v.reshape(shp).astype(f32), ig, lf)
    h_ml = rmsnorm(h_ml, ml_norm_g.reshape(ML_HEADS, ML_HEAD_DIM)) * jax.nn.sigmoid(ml_o.astype(f32)).reshape(shp)
    h_ml = h_ml.reshape(b_, s_, ML_WIDTH).astype(h.dtype)

    lam_init = 0.8 - 0.6 * math.exp(-0.3 * layer_idx)
    lam = (jnp.exp(jnp.sum(lambda_q1.astype(f32) * lambda_k1.astype(f32)))
           - jnp.exp(jnp.sum(lambda_q2.astype(f32) * lambda_k2.astype(f32))) + lam_init)
    qk_shp = (b_, s_, DA_HEADS, 2, DA_QK_DIM)
    h_da = diff_attention(da_q.reshape(qk_shp), da_k.reshape(qk_shp),
                          da_v.reshape(b_, s_, DA_HEADS, DA_VDIM), lam, lam_init, da_norm_g)
    h_da = h_da.reshape(b_, s_, DA_WIDTH).astype(h.dtype)

    return jnp.concatenate([h_ml, h_da], axis=-1) @ w_out


def cross_attention(h, mem_n, w_cq, w_ck, w_cv, w_co):
    b_, s_, _ = h.shape
    m_ = mem_n.shape[1]
    q = (h @ w_cq).reshape(b_, s_, CA_HEADS, CA_HEAD_DIM)
    k = (mem_n @ w_ck).reshape(b_, m_, CA_HEADS, CA_HEAD_DIM)
    v = (mem_n @ w_cv).reshape(b_, m_, CA_HEADS, CA_HEAD_DIM)
    sc = jnp.einsum('bqhd,bmhd->bhqm', q, k).astype(jnp.float32) * (CA_HEAD_DIM ** -0.5)
    p = jax.nn.softmax(sc, axis=-1).astype(v.dtype)
    o = jnp.einsum('bhqm,bmhd->bqhd', p, v).reshape(b_, s_, D_MODEL)
    return o @ w_co


def peer(h, w_pq, sub_keys, peer_u, peer_v):
    b_, s_, d_ = h.shape
    xt = h.reshape((b_ * s_) // PEER_BLOCK, PEER_BLOCK, d_)

    def block(xb):
        q = (xb @ w_pq).reshape(PEER_BLOCK, PEER_HEADS, 2, PEER_HALF)
        sc = jnp.einsum('thcd,hcnd->thcn', q, sub_keys).astype(jnp.float32)
        top_v, top_i = lax.top_k(sc, PEER_TOPK)
        cand_v = top_v[:, :, 0, :, None] + top_v[:, :, 1, None, :]
        cand_i = top_i[:, :, 0, :, None] * PEER_KEYS + top_i[:, :, 1, None, :]
        cand_v = cand_v.reshape(PEER_BLOCK, PEER_HEADS, PEER_TOPK * PEER_TOPK)
        cand_i = cand_i.reshape(PEER_BLOCK, PEER_HEADS, PEER_TOPK * PEER_TOPK)
        best_v, best_j = lax.top_k(cand_v, PEER_TOPK)
        eidx = jnp.take_along_axis(cand_i, best_j, axis=-1)
        gate = jax.nn.softmax(best_v, axis=-1)
        u = peer_u[eidx]
        act = jax.nn.gelu(jnp.einsum('td,thkd->thk', xb, u).astype(jnp.float32), approximate=False)
        coef = (gate * act).astype(xb.dtype)
        return jnp.einsum('thk,thkd->td', coef, peer_v[eidx])

    return lax.map(block, xt).reshape(b_, s_, d_)


def setup_inputs(seed: int = 0) -> dict:
    key = jax.random.key(seed)
    ks = jax.random.split(key, 32)
    f32 = jnp.float32

    def nrm(k, shape, scale):
        return jax.random.normal(k, shape, f32) * scale

    def gain(k, shape):
        return 1.0 + 0.01 * jax.random.normal(k, shape, f32)

    return {
        "x": nrm(ks[0], (BATCH, SEQ, D_MODEL), 1.0),
        "mem": nrm(ks[1], (BATCH, MEM_LEN, D_MODEL), 1.0),
        "norm_mix_g": gain(ks[2], (DEPTH, D_MODEL)),
        "w_in": nrm(ks[3], (DEPTH, D_MODEL, IN_WIDTH), D_MODEL ** -0.5),
        "conv_w": nrm(ks[4], (DEPTH, CONV_WIDTH, 2 * ML_WIDTH), CONV_WIDTH ** -0.5),
        "b_igate": nrm(ks[5], (DEPTH, ML_HEADS), 0.1),
        "b_fgate": jnp.broadcast_to(jnp.linspace(3.0, 6.0, ML_HEADS, dtype=f32), (DEPTH, ML_HEADS))
                    + nrm(ks[6], (DEPTH, ML_HEADS), 0.01),
        "ml_norm_g": gain(ks[7], (DEPTH, ML_WIDTH)),
        "lambda_q1": nrm(ks[8], (DEPTH, DA_QK_DIM), 0.1),
        "lambda_k1": nrm(ks[9], (DEPTH, DA_QK_DIM), 0.1),
        "lambda_q2": nrm(ks[10], (DEPTH, DA_QK_DIM), 0.1),
        "lambda_k2": nrm(ks[11], (DEPTH, DA_QK_DIM), 0.1),
        "da_norm_g": gain(ks[12], (DEPTH, DA_WIDTH)),
        "w_out": nrm(ks[13], (DEPTH, MIX_WIDTH, D_MODEL), MIX_WIDTH ** -0.5),
        "norm_ca_g": gain(ks[14], (DEPTH, D_MODEL)),
        "norm_mem_g": gain(ks[15], (DEPTH, D_MODEL)),
        "w_cq": nrm(ks[16], (DEPTH, D_MODEL, D_MODEL), D_MODEL ** -0.5),
        "w_ck": nrm(ks[17], (DEPTH, D_MODEL, D_MODEL), D_MODEL ** -0.5),
        "w_cv": nrm(ks[18], (DEPTH, D_MODEL, D_MODEL), D_MODEL ** -0.5),
        "w_co": nrm(ks[19], (DEPTH, D_MODEL, D_MODEL), D_MODEL ** -0.5),
        "norm_ffn_g": gain(ks[20], (DEPTH, D_MODEL)),
        "w_pq": nrm(ks[21], (DEPTH, D_MODEL, PEER_HEADS * PEER_DKEY), D_MODEL ** -0.5),
        "sub_keys": nrm(ks[22], (DEPTH, PEER_HEADS, 2, PEER_KEYS, PEER_HALF), PEER_HALF ** -0.5),
        "peer_u": nrm(ks[23], (DEPTH, PEER_EXPERTS, D_MODEL), D_MODEL ** -0.5),
        "peer_v": nrm(ks[24], (DEPTH, PEER_EXPERTS, D_MODEL), PEER_HEADS ** -0.5),
        "final_norm_g": gain(ks[25], (D_MODEL,)),
    }


def reference(x, mem, norm_mix_g, w_in, conv_w, b_igate, b_fgate, ml_norm_g, lambda_q1, lambda_k1,
              lambda_q2, lambda_k2, da_norm_g, w_out, norm_ca_g, norm_mem_g, w_cq, w_ck, w_cv, w_co,
              norm_ffn_g, w_pq, sub_keys, peer_u, peer_v, final_norm_g):
    for l in range(DEPTH):
        x = x + parallel_mixer(rmsnorm(x, norm_mix_g[l]), w_in[l], conv_w[l], b_igate[l], b_fgate[l],
                               ml_norm_g[l], lambda_q1[l], lambda_k1[l], lambda_q2[l], lambda_k2[l],
                               da_norm_g[l], w_out[l], l)
        x = x + cross_attention(rmsnorm(x, norm_ca_g[l]), rmsnorm(mem, norm_mem_g[l]),
                                w_cq[l], w_ck[l], w_cv[l], w_co[l])
        x = x + peer(rmsnorm(x, norm_ffn_g[l]), w_pq[l], sub_keys[l], peer_u[l], peer_v[l])
    return rmsnorm(x, final_norm_g)
```

```python
import functools
import math

import jax
import jax.numpy as jnp
import numpy as np
from jax import lax
from jax.experimental import pallas as pl
from jax.experimental.pallas import tpu as pltpu

F32 = jnp.float32
BF16 = jnp.bfloat16
EPS = 1e-6

ML_HEADS = 4
ML_HEAD_DIM = 128
ML_CHUNK = 64
CONV_WIDTH = 4
DA_HEADS = 4
DA_QK_DIM = 64
DA_VDIM = 128
CA_HEADS = 4
PEER_HEADS = 8
PEER_KEYS = 128
PEER_TOPK = 16
PEER_HALF = 64
LANES = 128

VMEM_LIMIT = 56 * 1024 * 1024


def _mm(a, b):
    return jnp.dot(a, b, preferred_element_type=F32)


def _mm_nt(a, b):
    return lax.dot_general(a, b, (((1,), (1,)), ((), ())), preferred_element_type=F32)


def _rms(x, g):
    return x * lax.rsqrt(jnp.mean(x * x, axis=-1, keepdims=True) + EPS) * g


def _params(sem):
    return pltpu.CompilerParams(dimension_semantics=sem, vmem_limit_bytes=VMEM_LIMIT)


def _inproj_kernel(x_ref, g_ref, w_ref, wg_ref, z_ref, gate_ref):
    hb = _rms(x_ref[...], g_ref[...]).astype(BF16)
    z_ref[...] = _mm(hb, w_ref[...]).astype(BF16)
    gate_ref[...] = _mm(hb, wg_ref[...])


def _inproj(x2d, g, w_main, w_gate, tm):
    n, d = x2d.shape
    wz = w_main.shape[1]
    return pl.pallas_call(
        _inproj_kernel,
        grid=(n // tm,),
        in_specs=[
            pl.BlockSpec((tm, d), lambda i: (i, 0)),
            pl.BlockSpec((1, d), lambda i: (0, 0)),
            pl.BlockSpec((d, wz), lambda i: (0, 0)),
            pl.BlockSpec((d, LANES), lambda i: (0, 0)),
        ],
        out_specs=[
            pl.BlockSpec((tm, wz), lambda i: (i, 0)),
            pl.BlockSpec((tm, LANES), lambda i: (i, 0)),
        ],
        out_shape=[
            jax.ShapeDtypeStruct((n, wz), BF16),
            jax.ShapeDtypeStruct((n, LANES), F32),
        ],
        compiler_params=_params(("parallel",)),
        name="inproj",
    )(x2d, g, w_main, w_gate)


def _mlstm_kernel(zqk_ref, zv_ref, zo_ref, gates_ref, convw_ref, gbias_ref, gml_ref, out_ref,
                  qk_scr, c_scr, *, seq):
    width = ML_HEADS * ML_HEAD_DIM
    conv_rows = 128
    halo = 16

    w = convw_ref[...]
    col = lax.broadcasted_iota(jnp.int32, (1, 2 * width), 1)
    kscale = jnp.where(col >= width, ML_HEAD_DIM ** -0.5, 1.0).astype(F32)

    def conv_body(rb, _):
        r0 = pl.multiple_of(rb * conv_rows, conv_rows)
        main = zqk_ref[pl.ds(r0, conv_rows), :].astype(F32)
        prev0 = pl.multiple_of(jnp.maximum(r0 - halo, 0), halo)
        prev = zqk_ref[pl.ds(prev0, halo), :].astype(F32)
        prev = jnp.where(rb > 0, prev, 0.0)
        xx = jnp.concatenate([prev, main], axis=0)
        acc = xx[halo:] * w[CONV_WIDTH - 1:CONV_WIDTH, :]
        for j in range(CONV_WIDTH - 1):
            sh = CONV_WIDTH - 1 - j
            acc = acc + pltpu.roll(xx, sh, 0)[halo:] * w[j:j + 1, :]
        y = acc * jax.nn.sigmoid(acc) * kscale
        qk_scr[pl.ds(r0, conv_rows), :] = y.astype(BF16)
        return 0

    lax.fori_loop(0, seq // conv_rows, conv_body, 0)

    c_scr[...] = jnp.zeros_like(c_scr)
    L = ML_CHUNK
    row = lax.broadcasted_iota(jnp.int32, (L, L), 0)
    colL = lax.broadcasted_iota(jnp.int32, (L, L), 1)
    tril = (colL <= row)
    tril_f = tril.astype(F32)
    lane128 = lax.broadcasted_iota(jnp.int32, (L, LANES), 1)
    ones_col = jnp.where(lane128 == 0, 1.0, 0.0).astype(BF16)
    gbias = gbias_ref[...]
    gml = gml_ref[...]

    def chunk_body(c, m_states):
        r0 = pl.multiple_of(c * L, L)
        gb = gates_ref[pl.ds(r0, L), :] + gbias
        lf = jnp.minimum(gb, 0.0) - jnp.log1p(jnp.exp(-jnp.abs(gb)))
        p = jnp.where(lane128 < ML_HEADS, gb, lf)
        cum = jnp.dot(tril_f, p, preferred_element_type=F32, precision=lax.Precision.HIGHEST)
        q_mix = jnp.where(lane128 < ML_HEADS, p, cum)
        qt = q_mix.T
        new_states = []
        for h in range(ML_HEADS):
            m_st = m_states[h]
            ic_col = p[:, h:h + 1]
            b_col = cum[:, ML_HEADS + h:ML_HEADS + h + 1]
            ic_row = qt[h:h + 1, :]
            b_row = qt[ML_HEADS + h:ML_HEADS + h + 1, :]
            b_last = b_col[L - 1:L, :]
            log_d = jnp.where(tril, b_col - b_row + ic_row, -jnp.inf)
            m_inter = b_col + m_st
            m_t = jnp.maximum(m_inter, jnp.max(log_d, axis=-1, keepdims=True))
            dmat = jnp.exp(log_d - m_t)
            hs = slice(h * ML_HEAD_DIM, (h + 1) * ML_HEAD_DIM)
            ks = slice(width + h * ML_HEAD_DIM, width + (h + 1) * ML_HEAD_DIM)
            qc = qk_scr[pl.ds(r0, L), hs]
            kc = qk_scr[pl.ds(r0, L), ks]
            vc = zv_ref[pl.ds(r0, L), hs]
            v_aug = jnp.concatenate([vc, ones_col], axis=1)
            sc = _mm_nt(qc, kc) * dmat
            inter = jnp.exp(m_inter - m_t)
            c_old = c_scr[h]
            num_aug = _mm(sc.astype(BF16), v_aug) + inter * _mm(qc, c_old.astype(BF16))
            num = num_aug[:, :ML_HEAD_DIM]
            den = num_aug[:, ML_HEAD_DIM:ML_HEAD_DIM + 1]
            hraw = num / jnp.maximum(jnp.abs(den), jnp.exp(-m_t))
            hn = _rms(hraw, gml[:, hs])
            og = zo_ref[pl.ds(r0, L), hs].astype(F32)
            out_ref[pl.ds(r0, L), hs] = (hn * jax.nn.sigmoid(og)).astype(BF16)
            g_col = b_last - b_col + ic_col
            m_next = jnp.maximum(b_last + m_st, jnp.max(g_col, axis=0, keepdims=True))
            decay = jnp.exp(b_last + m_st - m_next)
            wgt = jnp.exp(g_col - m_next)
            kw_t = (wgt * kc.astype(F32)).T.astype(BF16)
            c_scr[h] = decay * c_old + _mm(kw_t, v_aug)
            new_states.append(m_next)
        return tuple(new_states)

    init = tuple(jnp.zeros((1, 1), F32) for _ in range(ML_HEADS))
    lax.fori_loop(0, seq // L, chunk_body, init)


def _mlstm(z, gates, conv_w, gbias, gml, batch, seq):
    n = batch * seq
    width = ML_HEADS * ML_HEAD_DIM
    return pl.pallas_call(
        functools.partial(_mlstm_kernel, seq=seq),
        grid=(batch,),
        in_specs=[
            pl.BlockSpec((seq, 2 * width), lambda b: (b, 0)),
            pl.BlockSpec((seq, width), lambda b: (b, 2)),
            pl.BlockSpec((seq, width), lambda b: (b, 3)),
            pl.BlockSpec((seq, LANES), lambda b: (b, 0)),
            pl.BlockSpec((CONV_WIDTH, 2 * width), lambda b: (0, 0)),
            pl.BlockSpec((1, LANES), lambda b: (0, 0)),
            pl.BlockSpec((1, width), lambda b: (0, 0)),
        ],
        out_specs=pl.BlockSpec((seq, width), lambda b: (b, 0)),
        out_shape=jax.ShapeDtypeStruct((n, width), BF16),
        scratch_shapes=[
            pltpu.VMEM((seq, 2 * width), BF16),
            pltpu.VMEM((ML_HEADS, ML_HEAD_DIM, 2 * ML_HEAD_DIM), F32),
        ],
        compiler_params=_params(("parallel",)),
        name="mlstm",
    )(z, z, z, gates, conv_w, gbias, gml)


def _diffattn_kernel(slopes_ref, q_ref, k_ref, v_ref, lq1_ref, lk1_ref, lq2_ref, lk2_ref, g_ref,
                     o_ref, acc_ref, m_ref, l_ref, *, tq, lam_init):
    h = pl.program_id(1)
    i = pl.program_id(2)
    slope = slopes_ref[h]
    scale = DA_QK_DIM ** -0.5
    q = q_ref[...]
    lane = lax.broadcasted_iota(jnp.int32, q.shape, 1)
    zero = jnp.zeros_like(q)
    qm = (jnp.where(lane < DA_QK_DIM, q, zero), jnp.where(lane >= DA_QK_DIM, q, zero))
    m_ref[...] = jnp.full(m_ref.shape, -jnp.inf, F32)
    l_ref[...] = jnp.zeros(l_ref.shape, F32)
    acc_ref[...] = jnp.zeros(acc_ref.shape, F32)
    r = lax.broadcasted_iota(jnp.int32, (tq, tq), 0)
    c = lax.broadcasted_iota(jnp.int32, (tq, tq), 1)
    rel = (r - c).astype(F32) * slope

    def block(j, masked):
        k0 = pl.multiple_of(j * tq, tq)
        ks = k_ref[pl.ds(k0, tq), :]
        vs = v_ref[pl.ds(k0, tq), :]
        bias = rel + ((i - j) * tq).astype(F32) * slope
        for cc in range(2):
            s = _mm_nt(qm[cc], ks) * scale - bias
            if masked:
                s = jnp.where(c <= r, s, -jnp.inf)
            m_old = m_ref[cc]
            m_new = jnp.maximum(m_old, jnp.max(s, axis=-1, keepdims=True))
            alpha = jnp.exp(m_old - m_new)
            p = jnp.exp(s - m_new)
            l_ref[cc] = alpha * l_ref[cc] + jnp.sum(p, axis=-1, keepdims=True)
            acc_ref[cc] = alpha * acc_ref[cc] + _mm(p.astype(BF16), vs)
            m_ref[cc] = m_new

    def loop_body(j, carry):
        block(j, False)
        return carry

    lax.fori_loop(0, i, loop_body, 0)
    block(i, True)

    lam = (jnp.exp(jnp.sum(lq1_ref[...] * lk1_ref[...], axis=-1, keepdims=True))
           - jnp.exp(jnp.sum(lq2_ref[...] * lk2_ref[...], axis=-1, keepdims=True)) + lam_init)
    o = acc_ref[0] / l_ref[0] - lam * (acc_ref[1] / l_ref[1])
    o_ref[...] = (_rms(o, g_ref[...]) * (1.0 - lam_init)).astype(BF16)


def _diffattn(z, slopes, lq1, lk1, lq2, lk2, gda, batch, seq, tq, lam_init):
    n = batch * seq
    nq = seq // tq
    qcol0 = (4 * ML_HEADS * ML_HEAD_DIM) // LANES
    kcol0 = qcol0 + (DA_HEADS * DA_VDIM) // LANES
    vcol0 = kcol0 + (DA_HEADS * DA_VDIM) // LANES
    lam_spec = pl.BlockSpec((1, DA_QK_DIM), lambda b, h, i: (0, 0))
    return pl.pallas_call(
        functools.partial(_diffattn_kernel, tq=tq, lam_init=lam_init),
        grid=(batch, DA_HEADS, nq),
        in_specs=[
            pl.BlockSpec(memory_space=pltpu.SMEM),
            pl.BlockSpec((tq, LANES), lambda b, h, i: (b * nq + i, qcol0 + h)),
            pl.BlockSpec((seq, LANES), lambda b, h, i: (b, kcol0 + h)),
            pl.BlockSpec((seq, LANES), lambda b, h, i: (b, vcol0 + h)),
            lam_spec, lam_spec, lam_spec, lam_spec,
            pl.BlockSpec((1, DA_VDIM), lambda b, h, i: (0, h)),
        ],
        out_specs=pl.BlockSpec((tq, DA_VDIM), lambda b, h, i: (b * nq + i, h)),
        out_shape=jax.ShapeDtypeStruct((n, DA_HEADS * DA_VDIM), BF16),
        scratch_shapes=[
            pltpu.VMEM((2, tq, DA_VDIM), F32),
            pltpu.VMEM((2, tq, 1), F32),
            pltpu.VMEM((2, tq, 1), F32),
        ],
        compiler_params=_params(("parallel", "parallel", "arbitrary")),
        name="diffattn",
    )(slopes, z, z, z, lq1, lk1, lq2, lk2, gda)


def _outproj_kernel(x_ref, hml_ref, hda_ref, wo1_ref, wo2_ref, g_ref, wq_ref, x1_ref, q_ref):
    x1 = x_ref[...] + _mm(hml_ref[...], wo1_ref[...]) + _mm(hda_ref[...], wo2_ref[...])
    x1_ref[...] = x1
    q_ref[...] = _mm(_rms(x1, g_ref[...]).astype(BF16), wq_ref[...]).astype(BF16)


def _outproj(x2d, hml, hda, wo1, wo2, g, wq, tm):
    n, d = x2d.shape
    half = hml.shape[1]
    row = lambda i: (i, 0)
    fixed = lambda i: (0, 0)
    return pl.pallas_call(
        _outproj_kernel,
        grid=(n // tm,),
        in_specs=[
            pl.BlockSpec((tm, d), row),
            pl.BlockSpec((tm, half), row),
            pl.BlockSpec((tm, half), row),
            pl.BlockSpec((half, d), fixed),
            pl.BlockSpec((half, d), fixed),
            pl.BlockSpec((1, d), fixed),
            pl.BlockSpec((d, d), fixed),
        ],
        out_specs=[pl.BlockSpec((tm, d), row), pl.BlockSpec((tm, d), row)],
        out_shape=[jax.ShapeDtypeStruct((n, d), F32), jax.ShapeDtypeStruct((n, d), BF16)],
        compiler_params=_params(("parallel",)),
        name="outproj",
    )(x2d, hml, hda, wo1, wo2, g, wq)


def _memkv_kernel(m_ref, g_ref, wk_ref, wv_ref, k_ref, v_ref):
    mb = _rms(m_ref[...], g_ref[...]).astype(BF16)
    k_ref[...] = _mm(mb, wk_ref[...]).astype(BF16)
    v_ref[...] = _mm(mb, wv_ref[...]).astype(BF16)


def _memkv(mem2d, g, wk, wv, tm):
    n, d = mem2d.shape
    row = lambda i: (i, 0)
    fixed = lambda i: (0, 0)
    return pl.pallas_call(
        _memkv_kernel,
        grid=(n // tm,),
        in_specs=[pl.BlockSpec((tm, d), row), pl.BlockSpec((1, d), fixed),
                  pl.BlockSpec((d, d), fixed), pl.BlockSpec((d, d), fixed)],
        out_specs=[pl.BlockSpec((tm, d), row), pl.BlockSpec((tm, d), row)],
        out_shape=[jax.ShapeDtypeStruct((n, d), BF16), jax.ShapeDtypeStruct((n, d), BF16)],
        compiler_params=_params(("parallel",)),
        name="memkv",
    )(mem2d, g, wk, wv)


def _cross_kernel(q_ref, k_ref, v_ref, x1_ref, wco_ref, g_ref, wpq_ref, sk_ref,
                  x2_ref, xn_ref, st_ref):
    d = q_ref.shape[1]
    hd = d // CA_HEADS
    scale = hd ** -0.5
    outs = []
    for hh in range(CA_HEADS):
        cs = slice(hh * hd, (hh + 1) * hd)
        s = _mm_nt(q_ref[:, cs], k_ref[:, cs]) * scale
        s = s - jnp.max(s, axis=-1, keepdims=True)
        e = jnp.exp(s)
        p = e / jnp.sum(e, axis=-1, keepdims=True)
        outs.append(_mm(p.astype(BF16), v_ref[:, cs]).astype(BF16))
    o = jnp.concatenate(outs, axis=1)
    x2 = x1_ref[...] + _mm(o, wco_ref[...])
    x2_ref[...] = x2
    xn = _rms(x2, g_ref[...])
    xn_ref[...] = xn
    qp = _mm(xn.astype(BF16), wpq_ref[...]).astype(BF16)
    lane = lax.broadcasted_iota(jnp.int32, (qp.shape[0], LANES), 1)
    for hp in range(PEER_HEADS):
        qh = qp[:, hp * LANES:(hp + 1) * LANES]
        zero = jnp.zeros_like(qh)
        sk = sk_ref[hp]
        st_ref[2 * hp] = _mm_nt(sk, jnp.where(lane < PEER_HALF, qh, zero))
        st_ref[2 * hp + 1] = _mm_nt(sk, jnp.where(lane >= PEER_HALF, qh, zero))


def _cross(qca, kca, vca, x1, wco, g, wpq, skcat, tm, seq, mem_len):
    n, d = x1.shape
    per_b = seq // tm
    row = lambda i: (i, 0)
    fixed = lambda i: (0, 0)
    memb = lambda i: (i // per_b, 0)
    ng = 2 * PEER_HEADS
    return pl.pallas_call(
        _cross_kernel,
        grid=(n // tm,),
        in_specs=[
            pl.BlockSpec((tm, d), row),
            pl.BlockSpec((mem_len, d), memb),
            pl.BlockSpec((mem_len, d), memb),
            pl.BlockSpec((tm, d), row),
            pl.BlockSpec((d, d), fixed),
            pl.BlockSpec((1, d), fixed),
            pl.BlockSpec((d, PEER_HEADS * LANES), fixed),
            pl.BlockSpec((PEER_HEADS, PEER_KEYS, LANES), lambda i: (0, 0, 0)),
        ],
        out_specs=[
            pl.BlockSpec((tm, d), row),
            pl.BlockSpec((tm, d), row),
            pl.BlockSpec((ng, PEER_KEYS, tm), lambda i: (0, 0, i)),
        ],
        out_shape=[
            jax.ShapeDtypeStruct((n, d), F32),
            jax.ShapeDtypeStruct((n, d), F32),
            jax.ShapeDtypeStruct((ng, PEER_KEYS, n), F32),
        ],
        compiler_params=_params(("parallel",)),
        name="cross",
    )(qca, kca, vca, x1, wco, g, wpq, skcat)


def _extract_topk(x, payload):
    rows, t = x.shape
    rid = lax.broadcasted_iota(jnp.int32, (rows, t), 0)
    krow = lax.broadcasted_iota(jnp.int32, (PEER_TOPK, t), 0)
    vals = jnp.zeros((PEER_TOPK, t), F32)
    picked = jnp.zeros((PEER_TOPK, t), jnp.int32)
    for r in range(PEER_TOPK):
        m = jnp.max(x, axis=0, keepdims=True)
        idx = jnp.min(jnp.where(x == m, rid, rows), axis=0, keepdims=True)
        hit = rid == idx
        if payload is None:
            sel = idx
        else:
            sel = jnp.sum(jnp.where(hit, payload, 0), axis=0, keepdims=True)
        vals = jnp.where(krow == r, m, vals)
        picked = jnp.where(krow == r, sel, picked)
        x = jnp.where(hit, -jnp.inf, x)
    return vals, picked


def _route_kernel(st_ref, eidx_ref, gate_ref):
    gates = []
    eids = []
    for hp in range(PEER_HEADS):
        va, ia = _extract_topk(st_ref[2 * hp], None)
        vb, ib = _extract_topk(st_ref[2 * hp + 1], None)
        cand = jnp.concatenate([va[i:i + 1, :] + vb for i in range(PEER_TOPK)], axis=0)
        cid = jnp.concatenate([ia[i:i + 1, :] * PEER_KEYS + ib for i in range(PEER_TOPK)], axis=0)
        bv, be = _extract_topk(cand, cid)
        e = jnp.exp(bv - bv[0:1, :])
        gates.append(e / jnp.sum(e, axis=0, keepdims=True))
        eids.append(be)
    gate_ref[...] = jnp.concatenate(gates, axis=0).T
    eidx_ref[...] = jnp.concatenate(eids, axis=0).T


def _route(st, tt):
    ng, nk, n = st.shape
    width = PEER_HEADS * PEER_TOPK
    return pl.pallas_call(
        _route_kernel,
        grid=(n // tt,),
        in_specs=[pl.BlockSpec((ng, nk, tt), lambda i: (0, 0, i))],
        out_specs=[pl.BlockSpec((tt, width), lambda i: (i, 0)),
                   pl.BlockSpec((tt, width), lambda i: (i, 0))],
        out_shape=[jax.ShapeDtypeStruct((n, width), jnp.int32),
                   jax.ShapeDtypeStruct((n, width), F32)],
        compiler_params=_params(("parallel",)),
        name="route",
    )(st)


PEER_SLOTS = 4


def _peer_kernel(eidx_ref, gate_ref, xn_ref, x2_ref, gfin_ref, uv_hbm, out_ref, buf, sem, acc_ref, *, tb):
    nsel = PEER_HEADS * PEER_TOPK
    d = xn_ref.shape[1]

    def row_copy(t, k, slot):
        e = eidx_ref[t, k]
        return pltpu.make_async_copy(uv_hbm.at[pl.ds(e, 1), :], buf.at[slot, pl.ds(k, 1), :], sem.at[slot])

    def issue(t, slot):
        for k in range(nsel):
            row_copy(t, k, slot).start()

    def wait_slot(slot):
        pltpu.make_async_copy(uv_hbm.at[pl.ds(0, nsel), :], buf.at[slot], sem.at[slot]).wait()

    for s in range(PEER_SLOTS):
        issue(s, s)

    eye = (lax.broadcasted_iota(jnp.int32, (nsel, nsel), 0) == lax.broadcasted_iota(jnp.int32, (nsel, nsel), 1))

    def body(t, carry):
        slot = lax.rem(t, PEER_SLOTS)
        wait_slot(slot)
        x = xn_ref[pl.ds(t, 1), :]
        u = buf[slot, :, :d]
        act = jnp.sum(u * x, axis=-1, keepdims=True)
        g_row = gate_ref[pl.ds(t, 1), :]
        g_col = jnp.sum(jnp.where(eye, g_row, 0.0), axis=-1, keepdims=True)
        coef = 0.5 * act * (1.0 + lax.erf(act * (2.0 ** -0.5))) * g_col
        v = buf[slot, :, d:]
        acc_ref[pl.ds(t, 1), :] = jnp.sum(coef * v, axis=0, keepdims=True)

        @pl.when(t + PEER_SLOTS < tb)
        def _():
            issue(t + PEER_SLOTS, slot)

        return carry

    lax.fori_loop(0, tb, body, 0)
    out_ref[...] = _rms(x2_ref[...] + acc_ref[...], gfin_ref[...])


def _peer(eidx, gate, xn, x2, gfin, uv, tb):
    n, d = xn.shape
    nsel = PEER_HEADS * PEER_TOPK
    row = lambda i: (i, 0)
    return pl.pallas_call(
        functools.partial(_peer_kernel, tb=tb),
        grid=(n // tb,),
        in_specs=[
            pl.BlockSpec((tb, nsel), row, memory_space=pltpu.SMEM),
            pl.BlockSpec((tb, nsel), row),
            pl.BlockSpec((tb, d), row),
            pl.BlockSpec((tb, d), row),
            pl.BlockSpec((1, d), lambda i: (0, 0)),
            pl.BlockSpec(memory_space=pl.ANY),
        ],
        out_specs=pl.BlockSpec((tb, d), row),
        out_shape=jax.ShapeDtypeStruct((n, d), F32),
        scratch_shapes=[
            pltpu.VMEM((PEER_SLOTS, nsel, 2 * d), F32),
            pltpu.SemaphoreType.DMA((PEER_SLOTS,)),
            pltpu.VMEM((tb, d), F32),
        ],
        compiler_params=_params(("arbitrary",)),
        name="peer",
    )(eidx, gate, xn, x2, gfin, uv)


def _layer(x2d, mem2d, batch, seq, mem_len, layer_idx, norm_mix_g, w_in, conv_w, b_igate, b_fgate,
           ml_norm_g, lambda_q1, lambda_k1, lambda_q2, lambda_k2, da_norm_g, w_out, norm_ca_g,
           norm_mem_g, w_cq, w_ck, w_cv, w_co, norm_ffn_g, w_pq, sub_keys, peer_u, peer_v, final_g):
    n, d = x2d.shape
    mlw = ML_HEADS * ML_HEAD_DIM
    gate0 = 4 * mlw
    gate1 = gate0 + 2 * ML_HEADS
    tm = min(512, n)

    w_main = jnp.concatenate([w_in[:, :gate0], w_in[:, gate1:]], axis=1).astype(BF16)
    w_gate = jnp.pad(w_in[:, gate0:gate1], ((0, 0), (0, LANES - 2 * ML_HEADS))).astype(BF16)
    gbias = jnp.pad(jnp.concatenate([b_igate, b_fgate]), (0, LANES - 2 * ML_HEADS)).reshape(1, LANES)
    row = lambda a: a.reshape(1, -1)

    z, gates = _inproj(x2d, row(norm_mix_g), w_main, w_gate, tm)
    hml = _mlstm(z, gates, conv_w, gbias, row(ml_norm_g), batch, seq)

    lam_init = 0.8 - 0.6 * math.exp(-0.3 * layer_idx)
    slopes = jnp.asarray(np.array([2.0 ** (-8.0 * (i + 1) / DA_HEADS) for i in range(DA_HEADS)], np.float32))
    hda = _diffattn(z, slopes, row(lambda_q1), row(lambda_k1), row(lambda_q2), row(lambda_k2),
                    row(da_norm_g), batch, seq, min(256, seq), lam_init)

    wo = w_out.astype(BF16)
    x1, qca = _outproj(x2d, hml, hda, wo[:mlw], wo[mlw:], row(norm_ca_g), w_cq.astype(BF16), tm)
    kca, vca = _memkv(mem2d, row(norm_mem_g), w_ck.astype(BF16), w_cv.astype(BF16), min(512, mem2d.shape[0]))

    skcat = sub_keys.transpose(0, 2, 1, 3).reshape(PEER_HEADS, PEER_KEYS, 2 * PEER_HALF).astype(BF16)
    x2, xn, st = _cross(qca, kca, vca, x1, w_co.astype(BF16), row(norm_ffn_g), w_pq.astype(BF16), skcat,
                        min(256, seq), seq, mem_len)
    eidx, gate = _route(st, min(256, n))
    uv = jnp.concatenate([peer_u, peer_v], axis=1)
    return _peer(eidx, gate, xn, x2, row(final_g), uv, min(128, n))


def kernel(x, mem, norm_mix_g, w_in, conv_w, b_igate, b_fgate, ml_norm_g, lambda_q1, lambda_k1, lambda_q2, lambda_k2, da_norm_g, w_out, norm_ca_g, norm_mem_g, w_cq, w_ck, w_cv, w_co, norm_ffn_g, w_pq, sub_keys, peer_u, peer_v, final_norm_g):
    batch, seq, d = x.shape
    mem_len = mem.shape[1]
    depth = w_in.shape[0]
    assert depth == 1, "final norm is fused into the last layer's PEER kernel; one layer supported"
    x2d = x.reshape(batch * seq, d)
    mem2d = mem.reshape(batch * mem_len, d)
    out = _layer(x2d, mem2d, batch, seq, mem_len, 0, norm_mix_g[0], w_in[0], conv_w[0], b_igate[0],
                 b_fgate[0], ml_norm_g[0], lambda_q1[0], lambda_k1[0], lambda_q2[0], lambda_k2[0],
                 da_norm_g[0], w_out[0], norm_ca_g[0], norm_mem_g[0], w_cq[0], w_ck[0], w_cv[0],
                 w_co[0], norm_ffn_g[0], w_pq[0], sub_keys[0], peer_u[0], peer_v[0], final_norm_g)
    return out.reshape(batch, seq, d)
```

```python
import functools
import math

import jax
import jax.numpy as jnp
import numpy as np
from jax import lax
from jax.experimental import pallas as pl
from jax.experimental.pallas import tpu as pltpu

F32 = jnp.float32
BF16 = jnp.bfloat16
EPS = 1e-6

ML_HEADS = 4
ML_HEAD_DIM = 128
ML_CHUNK = 64
CONV_WIDTH = 4
DA_HEADS = 4
DA_QK_DIM = 64
DA_VDIM = 128
CA_HEADS = 4
PEER_HEADS = 8
PEER_KEYS = 128
PEER_TOPK = 16
PEER_HALF = 64
LANES = 128

VMEM_LIMIT = 56 * 1024 * 1024


def _mm(a, b):
    return jnp.dot(a, b, preferred_element_type=F32)


def _mm_nt(a, b):
    return lax.dot_general(a, b, (((1,), (1,)), ((), ())), preferred_element_type=F32)


def _rms(x, g):
    return x * lax.rsqrt(jnp.mean(x * x, axis=-1, keepdims=True) + EPS) * g


def _params(sem):
    return pltpu.CompilerParams(dimension_semantics=sem, vmem_limit_bytes=VMEM_LIMIT)


def _inproj_kernel(x_ref, g_ref, w_ref, wg_ref, z_ref, gate_ref):
    hb = _rms(x_ref[...], g_ref[...]).astype(BF16)
    z_ref[...] = _mm(hb, w_ref[...]).astype(BF16)
    gate_ref[...] = _mm(hb, wg_ref[...])


def _inproj(x2d, g, w_main, w_gate, tm):
    n, d = x2d.shape
    wz = w_main.shape[1]
    return pl.pallas_call(
        _inproj_kernel,
        grid=(n // tm,),
        in_specs=[
            pl.BlockSpec((tm, d), lambda i: (i, 0)),
            pl.BlockSpec((1, d), lambda i: (0, 0)),
            pl.BlockSpec((d, wz), lambda i: (0, 0)),
            pl.BlockSpec((d, LANES), lambda i: (0, 0)),
        ],
        out_specs=[
            pl.BlockSpec((tm, wz), lambda i: (i, 0)),
            pl.BlockSpec((tm, LANES), lambda i: (i, 0)),
        ],
        out_shape=[
            jax.ShapeDtypeStruct((n, wz), BF16),
            jax.ShapeDtypeStruct((n, LANES), F32),
        ],
        compiler_params=_params(("parallel",)),
        name="inproj",
    )(x2d, g, w_main, w_gate)


def _mlstm_kernel(zqk_ref, zv_ref, zo_ref, gates_ref, convw_ref, gbias_ref, gml_ref, out_ref,
                  qk_scr, c_scr, *, seq):
    width = ML_HEADS * ML_HEAD_DIM
    conv_rows = 128
    halo = 16

    w = convw_ref[...]
    col = lax.broadcasted_iota(jnp.int32, (1, 2 * width), 1)
    kscale = jnp.where(col >= width, ML_HEAD_DIM ** -0.5, 1.0).astype(F32)

    def conv_body(rb, _):
        r0 = pl.multiple_of(rb * conv_rows, conv_rows)
        main = zqk_ref[pl.ds(r0, conv_rows), :].astype(F32)
        prev0 = pl.multiple_of(jnp.maximum(r0 - halo, 0), halo)
        prev = zqk_ref[pl.ds(prev0, halo), :].astype(F32)
        prev = jnp.where(rb > 0, prev, 0.0)
        xx = jnp.concatenate([prev, main], axis=0)
        acc = xx[halo:] * w[CONV_WIDTH - 1:CONV_WIDTH, :]
        for j in range(CONV_WIDTH - 1):
            sh = CONV_WIDTH - 1 - j
            acc = acc + pltpu.roll(xx, sh, 0)[halo:] * w[j:j + 1, :]
        y = acc * jax.nn.sigmoid(acc) * kscale
        qk_scr[pl.ds(r0, conv_rows), :] = y.astype(BF16)
        return 0

    lax.fori_loop(0, seq // conv_rows, conv_body, 0)

    c_scr[...] = jnp.zeros_like(c_scr)
    L = ML_CHUNK
    row = lax.broadcasted_iota(jnp.int32, (L, L), 0)
    colL = lax.broadcasted_iota(jnp.int32, (L, L), 1)
    tril = (colL <= row)
    tril_f = tril.astype(F32)
    lane128 = lax.broadcasted_iota(jnp.int32, (L, LANES), 1)
    ones_col = jnp.where(lane128 == 0, 1.0, 0.0).astype(BF16)
    gbias = gbias_ref[...]
    gml = gml_ref[...]

    def chunk_body(c, m_states):
        r0 = pl.multiple_of(c * L, L)
        gb = gates_ref[pl.ds(r0, L), :] + gbias
        lf = jnp.minimum(gb, 0.0) - jnp.log1p(jnp.exp(-jnp.abs(gb)))
        p = jnp.where(lane128 < ML_HEADS, gb, lf)
        cum = jnp.dot(tril_f, p, preferred_element_type=F32, precision=lax.Precision.HIGHEST)
        q_mix = jnp.where(lane128 < ML_HEADS, p, cum)
        qt = q_mix.T
        new_states = []
        for h in range(ML_HEADS):
            m_st = m_states[h]
            ic_col = p[:, h:h + 1]
            b_col = cum[:, ML_HEADS + h:ML_HEADS + h + 1]
            ic_row = qt[h:h + 1, :]
            b_row = qt[ML_HEADS + h:ML_HEADS + h + 1, :]
            b_last = b_col[L - 1:L, :]
            log_d = jnp.where(tril, b_col - b_row + ic_row, -jnp.inf)
            m_inter = b_col + m_st
            m_t = jnp.maximum(m_inter, jnp.max(log_d, axis=-1, keepdims=True))
            dmat = jnp.exp(log_d - m_t)
            hs = slice(h * ML_HEAD_DIM, (h + 1) * ML_HEAD_DIM)
            ks = slice(width + h * ML_HEAD_DIM, width + (h + 1) * ML_HEAD_DIM)
            qc = qk_scr[pl.ds(r0, L), hs]
            kc = qk_scr[pl.ds(r0, L), ks]
            vc = zv_ref[pl.ds(r0, L), hs]
            v_aug = jnp.concatenate([vc, ones_col], axis=1)
            sc = _mm_nt(qc, kc) * dmat
            inter = jnp.exp(m_inter - m_t)
            c_old = c_scr[h]
            num_aug = _mm(sc.astype(BF16), v_aug) + inter * _mm(qc, c_old.astype(BF16))
            num = num_aug[:, :ML_HEAD_DIM]
            den = num_aug[:, ML_HEAD_DIM:ML_HEAD_DIM + 1]
            hraw = num / jnp.maximum(jnp.abs(den), jnp.exp(-m_t))
            hn = _rms(hraw, gml[:, hs])
            og = zo_ref[pl.ds(r0, L), hs].astype(F32)
            out_ref[pl.ds(r0, L), hs] = (hn * jax.nn.sigmoid(og)).astype(BF16)
            g_col = b_last - b_col + ic_col
            m_next = jnp.maximum(b_last + m_st, jnp.max(g_col, axis=0, keepdims=True))
            decay = jnp.exp(b_last + m_st - m_next)
            wgt = jnp.exp(g_col - m_next)
            kw_t = (wgt * kc.astype(F32)).T.astype(BF16)
            c_scr[h] = decay * c_old + _mm(kw_t, v_aug)
            new_states.append(m_next)
        return tuple(new_states)

    init = tuple(jnp.zeros((1, 1), F32) for _ in range(ML_HEADS))
    lax.fori_loop(0, seq // L, chunk_body, init)


def _mlstm(z, gates, conv_w, gbias, gml, batch, seq):
    n = batch * seq
    width = ML_HEADS * ML_HEAD_DIM
    return pl.pallas_call(
        functools.partial(_mlstm_kernel, seq=seq),
        grid=(batch,),
        in_specs=[
            pl.BlockSpec((seq, 2 * width), lambda b: (b, 0)),
            pl.BlockSpec((seq, width), lambda b: (b, 2)),
            pl.BlockSpec((seq, width), lambda b: (b, 3)),
            pl.BlockSpec((seq, LANES), lambda b: (b, 0)),
            pl.BlockSpec((CONV_WIDTH, 2 * width), lambda b: (0, 0)),
            pl.BlockSpec((1, LANES), lambda b: (0, 0)),
            pl.BlockSpec((1, width), lambda b: (0, 0)),
        ],
        out_specs=pl.BlockSpec((seq, width), lambda b: (b, 0)),
        out_shape=jax.ShapeDtypeStruct((n, width), BF16),
        scratch_shapes=[
            pltpu.VMEM((seq, 2 * width), BF16),
            pltpu.VMEM((ML_HEADS, ML_HEAD_DIM, 2 * ML_HEAD_DIM), F32),
        ],
        compiler_params=_params(("parallel",)),
        name="mlstm",
    )(z, z, z, gates, conv_w, gbias, gml)


def _diffattn_kernel(slopes_ref, q_ref, k_ref, v_ref, lq1_ref, lk1_ref, lq2_ref, lk2_ref, g_ref,
                     o_ref, acc_ref, m_ref, l_ref, *, tq, lam_init):
    h = pl.program_id(1)
    i = pl.program_id(2)
    slope = slopes_ref[h]
    scale = DA_QK_DIM ** -0.5
    q = q_ref[...]
    lane = lax.broadcasted_iota(jnp.int32, q.shape, 1)
    zero = jnp.zeros_like(q)
    qm = (jnp.where(lane < DA_QK_DIM, q, zero), jnp.where(lane >= DA_QK_DIM, q, zero))
    m_ref[...] = jnp.full(m_ref.shape, -jnp.inf, F32)
    l_ref[...] = jnp.zeros(l_ref.shape, F32)
    acc_ref[...] = jnp.zeros(acc_ref.shape, F32)
    r = lax.broadcasted_iota(jnp.int32, (tq, tq), 0)
    c = lax.broadcasted_iota(jnp.int32, (tq, tq), 1)
    rel = (r - c).astype(F32) * slope

    def block(j, masked):
        k0 = pl.multiple_of(j * tq, tq)
        ks = k_ref[pl.ds(k0, tq), :]
        vs = v_ref[pl.ds(k0, tq), :]
        bias = rel + ((i - j) * tq).astype(F32) * slope
        for cc in range(2):
            s = _mm_nt(qm[cc], ks) * scale - bias
            if masked:
                s = jnp.where(c <= r, s, -jnp.inf)
            m_old = m_ref[cc]
            m_new = jnp.maximum(m_old, jnp.max(s, axis=-1, keepdims=True))
            alpha = jnp.exp(m_old - m_new)
            p = jnp.exp(s - m_new)
            l_ref[cc] = alpha * l_ref[cc] + jnp.sum(p, axis=-1, keepdims=True)
            acc_ref[cc] = alpha * acc_ref[cc] + _mm(p.astype(BF16), vs)
            m_ref[cc] = m_new

    def loop_body(j, carry):
        block(j, False)
        return carry

    lax.fori_loop(0, i, loop_body, 0)
    block(i, True)

    lam = (jnp.exp(jnp.sum(lq1_ref[...] * lk1_ref[...], axis=-1, keepdims=True))
           - jnp.exp(jnp.sum(lq2_ref[...] * lk2_ref[...], axis=-1, keepdims=True)) + lam_init)
    o = acc_ref[0] / l_ref[0] - lam * (acc_ref[1] / l_ref[1])
    o_ref[...] = (_rms(o, g_ref[...]) * (1.0 - lam_init)).astype(BF16)


def _diffattn(z, slopes, lq1, lk1, lq2, lk2, gda, batch, seq, tq, lam_init):
    n = batch * seq
    nq = seq // tq
    qcol0 = (4 * ML_HEADS * ML_HEAD_DIM) // LANES
    kcol0 = qcol0 + (DA_HEADS * DA_VDIM) // LANES
    vcol0 = kcol0 + (DA_HEADS * DA_VDIM) // LANES
    lam_spec = pl.BlockSpec((1, DA_QK_DIM), lambda b, h, i: (0, 0))
    return pl.pallas_call(
        functools.partial(_diffattn_kernel, tq=tq, lam_init=lam_init),
        grid=(batch, DA_HEADS, nq),
        in_specs=[
            pl.BlockSpec(memory_space=pltpu.SMEM),
            pl.BlockSpec((tq, LANES), lambda b, h, i: (b * nq + i, qcol0 + h)),
            pl.BlockSpec((seq, LANES), lambda b, h, i: (b, kcol0 + h)),
            pl.BlockSpec((seq, LANES), lambda b, h, i: (b, vcol0 + h)),
            lam_spec, lam_spec, lam_spec, lam_spec,
            pl.BlockSpec((1, DA_VDIM), lambda b, h, i: (0, h)),
        ],
        out_specs=pl.BlockSpec((tq, DA_VDIM), lambda b, h, i: (b * nq + i, h)),
        out_shape=jax.ShapeDtypeStruct((n, DA_HEADS * DA_VDIM), BF16),
        scratch_shapes=[
            pltpu.VMEM((2, tq, DA_VDIM), F32),
            pltpu.VMEM((2, tq, 1), F32),
            pltpu.VMEM((2, tq, 1), F32),
        ],
        compiler_params=_params(("parallel", "parallel", "arbitrary")),
        name="diffattn",
    )(slopes, z, z, z, lq1, lk1, lq2, lk2, gda)


def _outproj_kernel(x_ref, hml_ref, hda_ref, wo1_ref, wo2_ref, g_ref, wq_ref, x1_ref, q_ref):
    x1 = x_ref[...] + _mm(hml_ref[...], wo1_ref[...]) + _mm(hda_ref[...], wo2_ref[...])
    x1_ref[...] = x1
    q_ref[...] = _mm(_rms(x1, g_ref[...]).astype(BF16), wq_ref[...]).astype(BF16)


def _outproj(x2d, hml, hda, wo1, wo2, g, wq, tm):
    n, d = x2d.shape
    half = hml.shape[1]
    row = lambda i: (i, 0)
    fixed = lambda i: (0, 0)
    return pl.pallas_call(
        _outproj_kernel,
        grid=(n // tm,),
        in_specs=[
            pl.BlockSpec((tm, d), row),
            pl.BlockSpec((tm, half), row),
            pl.BlockSpec((tm, half), row),
            pl.BlockSpec((half, d), fixed),
            pl.BlockSpec((half, d), fixed),
            pl.BlockSpec((1, d), fixed),
            pl.BlockSpec((d, d), fixed),
        ],
        out_specs=[pl.BlockSpec((tm, d), row), pl.BlockSpec((tm, d), row)],
        out_shape=[jax.ShapeDtypeStruct((n, d), F32), jax.ShapeDtypeStruct((n, d), BF16)],
        compiler_params=_params(("parallel",)),
        name="outproj",
    )(x2d, hml, hda, wo1, wo2, g, wq)


def _memkv_kernel(m_ref, g_ref, wk_ref, wv_ref, k_ref, v_ref):
    mb = _rms(m_ref[...], g_ref[...]).astype(BF16)
    k_ref[...] = _mm(mb, wk_ref[...]).astype(BF16)
    v_ref[...] = _mm(mb, wv_ref[...]).astype(BF16)


def _memkv(mem2d, g, wk, wv, tm):
    n, d = mem2d.shape
    row = lambda i: (i, 0)
    fixed = lambda i: (0, 0)
    return pl.pallas_call(
        _memkv_kernel,
        grid=(n // tm,),
        in_specs=[pl.BlockSpec((tm, d), row), pl.BlockSpec((1, d), fixed),
                  pl.BlockSpec((d, d), fixed), pl.BlockSpec((d, d), fixed)],
        out_specs=[pl.BlockSpec((tm, d), row), pl.BlockSpec((tm, d), row)],
        out_shape=[jax.ShapeDtypeStruct((n, d), BF16), jax.ShapeDtypeStruct((n, d), BF16)],
        compiler_params=_params(("parallel",)),
        name="memkv",
    )(mem2d, g, wk, wv)


def _cross_kernel(q_ref, k_ref, v_ref, x1_ref, wco_ref, g_ref, wpq_ref, sk_ref,
                  x2_ref, xn_ref, st_ref):
    d = q_ref.shape[1]
    hd = d // CA_HEADS
    scale = hd ** -0.5
    outs = []
    for hh in range(CA_HEADS):
        cs = slice(hh * hd, (hh + 1) * hd)
        s = _mm_nt(q_ref[:, cs], k_ref[:, cs]) * scale
        s = s - jnp.max(s, axis=-1, keepdims=True)
        e = jnp.exp(s)
        p = e / jnp.sum(e, axis=-1, keepdims=True)
        outs.append(_mm(p.astype(BF16), v_ref[:, cs]).astype(BF16))
    o = jnp.concatenate(outs, axis=1)
    x2 = x1_ref[...] + _mm(o, wco_ref[...])
    x2_ref[...] = x2
    xn = _rms(x2, g_ref[...])
    xn_ref[...] = xn
    qp = _mm(xn.astype(BF16), wpq_ref[...]).astype(BF16)
    lane = lax.broadcasted_iota(jnp.int32, (qp.shape[0], LANES), 1)
    for hp in range(PEER_HEADS):
        qh = qp[:, hp * LANES:(hp + 1) * LANES]
        zero = jnp.zeros_like(qh)
        sk = sk_ref[hp]
        st_ref[2 * hp] = _mm_nt(sk, jnp.where(lane < PEER_HALF, qh, zero))
        st_ref[2 * hp + 1] = _mm_nt(sk, jnp.where(lane >= PEER_HALF, qh, zero))


def _cross(qca, kca, vca, x1, wco, g, wpq, skcat, tm, seq, mem_len):
    n, d = x1.shape
    per_b = seq // tm
    row = lambda i: (i, 0)
    fixed = lambda i: (0, 0)
    memb = lambda i: (i // per_b, 0)
    ng = 2 * PEER_HEADS
    return pl.pallas_call(
        _cross_kernel,
        grid=(n // tm,),
        in_specs=[
            pl.BlockSpec((tm, d), row),
            pl.BlockSpec((mem_len, d), memb),
            pl.BlockSpec((mem_len, d), memb),
            pl.BlockSpec((tm, d), row),
            pl.BlockSpec((d, d), fixed),
            pl.BlockSpec((1, d), fixed),
            pl.BlockSpec((d, PEER_HEADS * LANES), fixed),
            pl.BlockSpec((PEER_HEADS, PEER_KEYS, LANES), lambda i: (0, 0, 0)),
        ],
        out_specs=[
            pl.BlockSpec((tm, d), row),
            pl.BlockSpec((tm, d), row),
            pl.BlockSpec((ng, PEER_KEYS, tm), lambda i: (0, 0, i)),
        ],
        out_shape=[
            jax.ShapeDtypeStruct((n, d), F32),
            jax.ShapeDtypeStruct((n, d), F32),
            jax.ShapeDtypeStruct((ng, PEER_KEYS, n), F32),
        ],
        compiler_params=_params(("parallel",)),
        name="cross",
    )(qca, kca, vca, x1, wco, g, wpq, skcat)


def _extract_topk(x, payload):
    rows, t = x.shape
    rid = lax.broadcasted_iota(jnp.int32, (rows, t), 0)
    krow = lax.broadcasted_iota(jnp.int32, (PEER_TOPK, t), 0)
    vals = jnp.zeros((PEER_TOPK, t), F32)
    picked = jnp.zeros((PEER_TOPK, t), jnp.int32)
    for r in range(PEER_TOPK):
        m = jnp.max(x, axis=0, keepdims=True)
        idx = jnp.min(jnp.where(x == m, rid, rows), axis=0, keepdims=True)
        hit = rid == idx
        if payload is None:
            sel = idx
        else:
            sel = jnp.sum(jnp.where(hit, payload, 0), axis=0, keepdims=True)
        vals = jnp.where(krow == r, m, vals)
        picked = jnp.where(krow == r, sel, picked)
        x = jnp.where(hit, -jnp.inf, x)
    return vals, picked


def _route_kernel(st_ref, eidx_ref, gate_ref):
    gates = []
    eids = []
    for hp in range(PEER_HEADS):
        va, ia = _extract_topk(st_ref[2 * hp], None)
        vb, ib = _extract_topk(st_ref[2 * hp + 1], None)
        cand = jnp.concatenate([va[i:i + 1, :] + vb for i in range(PEER_TOPK)], axis=0)
        cid = jnp.concatenate([ia[i:i + 1, :] * PEER_KEYS + ib for i in range(PEER_TOPK)], axis=0)
        bv, be = _extract_topk(cand, cid)
        e = jnp.exp(bv - bv[0:1, :])
        gates.append(e / jnp.sum(e, axis=0, keepdims=True))
        eids.append(be)
    gate_ref[...] = jnp.concatenate(gates, axis=0).T
    eidx_ref[...] = jnp.concatenate(eids, axis=0).T


def _route(st, tt):
    ng, nk, n = st.shape
    width = PEER_HEADS * PEER_TOPK
    return pl.pallas_call(
        _route_kernel,
        grid=(n // tt,),
        in_specs=[pl.BlockSpec((ng, nk, tt), lambda i: (0, 0, i))],
        out_specs=[pl.BlockSpec((tt, width), lambda i: (i, 0)),
                   pl.BlockSpec((tt, width), lambda i: (i, 0))],
        out_shape=[jax.ShapeDtypeStruct((n, width), jnp.int32),
                   jax.ShapeDtypeStruct((n, width), F32)],
        compiler_params=_params(("parallel",)),
        name="route",
    )(st)


PEER_SLOTS = 8


def _peer_kernel(eidx_ref, enext_ref, gate_ref, xn_ref, x2_ref, gfin_ref, uv_hbm, out_ref, buf, sem, acc_ref, *, tb):
    nsel = PEER_HEADS * PEER_TOPK
    d = xn_ref.shape[1]
    i = pl.program_id(0)
    last_step = pl.num_programs(0) - 1
    ngroups = tb // PEER_SLOTS

    def issue(idx_ref, t, slot):
        for k in range(nsel):
            e = idx_ref[t, k]
            pltpu.make_async_copy(uv_hbm.at[e], buf.at[slot, pl.ds(k, 1), :], sem.at[slot]).start(priority=k % 2)

    def wait_slot(slot):
        pltpu.make_async_copy(buf.at[slot], buf.at[slot], sem.at[slot]).wait()

    eye = (lax.broadcasted_iota(jnp.int32, (nsel, nsel), 0) == lax.broadcasted_iota(jnp.int32, (nsel, nsel), 1))

    def compute(t, s):
        x = xn_ref[pl.ds(t, 1), :]
        u = buf[s, :, :d]
        act = jnp.sum(u * x, axis=-1, keepdims=True)
        g_row = gate_ref[pl.ds(t, 1), :]
        g_col = jnp.sum(jnp.where(eye, g_row, 0.0), axis=-1, keepdims=True)
        coef = 0.5 * act * (1.0 + lax.erf(act * (2.0 ** -0.5))) * g_col
        v = buf[s, :, d:]
        return jnp.sum(coef * v, axis=0, keepdims=True)

    @pl.when(i == 0)
    def _():
        for s in range(PEER_SLOTS - 1):
            issue(eidx_ref, s, s)

    def group(g, carry):
        for s in range(PEER_SLOTS):
            t = g * PEER_SLOTS + s
            wait_slot(s)
            row = compute(t, s)
            issue(eidx_ref, t + PEER_SLOTS - 1, (s - 1) % PEER_SLOTS)
            acc_ref[pl.ds(t, 1), :] = row
        return carry

    lax.fori_loop(0, ngroups - 1, group, 0)

    for s in range(PEER_SLOTS):
        t = (ngroups - 1) * PEER_SLOTS + s
        wait_slot(s)
        row = compute(t, s)
        if s == 0:
            issue(eidx_ref, t + PEER_SLOTS - 1, PEER_SLOTS - 1)
        else:
            @pl.when(i < last_step)
            def _():
                issue(enext_ref, s - 1, s - 1)
        acc_ref[pl.ds(t, 1), :] = row

    out_ref[...] = _rms(x2_ref[...] + acc_ref[...], gfin_ref[...])


def _peer(eidx, gate, xn, x2, gfin, uv, tb):
    n, d = xn.shape
    nsel = PEER_HEADS * PEER_TOPK
    nsteps = n // tb
    row = lambda i: (i, 0)
    nxt = lambda i: (jnp.minimum(i + 1, nsteps - 1) * (tb // PEER_SLOTS), 0)
    return pl.pallas_call(
        functools.partial(_peer_kernel, tb=tb),
        grid=(nsteps,),
        in_specs=[
            pl.BlockSpec((tb, nsel), row, memory_space=pltpu.SMEM),
            pl.BlockSpec((PEER_SLOTS, nsel), nxt, memory_space=pltpu.SMEM),
            pl.BlockSpec((tb, nsel), row),
            pl.BlockSpec((tb, d), row),
            pl.BlockSpec((tb, d), row),
            pl.BlockSpec((1, d), lambda i: (0, 0)),
            pl.BlockSpec(memory_space=pl.ANY),
        ],
        out_specs=pl.BlockSpec((tb, d), row),
        out_shape=jax.ShapeDtypeStruct((n, d), F32),
        scratch_shapes=[
            pltpu.VMEM((PEER_SLOTS, nsel, 2 * d), F32),
            pltpu.SemaphoreType.DMA((PEER_SLOTS,)),
            pltpu.VMEM((tb, d), F32),
        ],
        compiler_params=_params(("arbitrary",)),
        name="peer",
    )(eidx, eidx, gate, xn, x2, gfin, uv)


def _layer(x2d, mem2d, batch, seq, mem_len, layer_idx, norm_mix_g, w_in, conv_w, b_igate, b_fgate,
           ml_norm_g, lambda_q1, lambda_k1, lambda_q2, lambda_k2, da_norm_g, w_out, norm_ca_g,
           norm_mem_g, w_cq, w_ck, w_cv, w_co, norm_ffn_g, w_pq, sub_keys, peer_u, peer_v, final_g):
    n, d = x2d.shape
    mlw = ML_HEADS * ML_HEAD_DIM
    gate0 = 4 * mlw
    gate1 = gate0 + 2 * ML_HEADS
    tm = min(512, n)

    w_main = jnp.concatenate([w_in[:, :gate0], w_in[:, gate1:]], axis=1).astype(BF16)
    w_gate = jnp.pad(w_in[:, gate0:gate1], ((0, 0), (0, LANES - 2 * ML_HEADS))).astype(BF16)
    gbias = jnp.pad(jnp.concatenate([b_igate, b_fgate]), (0, LANES - 2 * ML_HEADS)).reshape(1, LANES)
    row = lambda a: a.reshape(1, -1)

    z, gates = _inproj(x2d, row(norm_mix_g), w_main, w_gate, tm)
    hml = _mlstm(z, gates, conv_w, gbias, row(ml_norm_g), batch, seq)

    lam_init = 0.8 - 0.6 * math.exp(-0.3 * layer_idx)
    slopes = jnp.asarray(np.array([2.0 ** (-8.0 * (i + 1) / DA_HEADS) for i in range(DA_HEADS)], np.float32))
    hda = _diffattn(z, slopes, row(lambda_q1), row(lambda_k1), row(lambda_q2), row(lambda_k2),
                    row(da_norm_g), batch, seq, min(256, seq), lam_init)

    wo = w_out.astype(BF16)
    x1, qca = _outproj(x2d, hml, hda, wo[:mlw], wo[mlw:], row(norm_ca_g), w_cq.astype(BF16), tm)
    kca, vca = _memkv(mem2d, row(norm_mem_g), w_ck.astype(BF16), w_cv.astype(BF16), min(512, mem2d.shape[0]))

    skcat = sub_keys.transpose(0, 2, 1, 3).reshape(PEER_HEADS, PEER_KEYS, 2 * PEER_HALF).astype(BF16)
    x2, xn, st = _cross(qca, kca, vca, x1, w_co.astype(BF16), row(norm_ffn_g), w_pq.astype(BF16), skcat,
                        min(256, seq), seq, mem_len)
    eidx, gate = _route(st, min(256, n))
    uv = jnp.concatenate([peer_u, peer_v], axis=1).reshape(peer_u.shape[0], 1, 2 * d)
    return _peer(eidx, gate, xn, x2, row(final_g), uv, min(128, n))


def kernel(x, mem, norm_mix_g, w_in, conv_w, b_igate, b_fgate, ml_norm_g, lambda_q1, lambda_k1, lambda_q2, lambda_k2, da_norm_g, w_out, norm_ca_g, norm_mem_g, w_cq, w_ck, w_cv, w_co, norm_ffn_g, w_pq, sub_keys, peer_u, peer_v, final_norm_g):
    batch, seq, d = x.shape
    mem_len = mem.shape[1]
    depth = w_in.shape[0]
    assert depth == 1, "final norm is fused into the last layer's PEER kernel; one layer supported"
    x2d = x.reshape(batch * seq, d)
    mem2d = mem.reshape(batch * mem_len, d)
    out = _layer(x2d, mem2d, batch, seq, mem_len, 0, norm_mix_g[0], w_in[0], conv_w[0], b_igate[0],
                 b_fgate[0], ml_norm_g[0], lambda_q1[0], lambda_k1[0], lambda_q2[0], lambda_k2[0],
                 da_norm_g[0], w_out[0], norm_ca_g[0], norm_mem_g[0], w_cq[0], w_ck[0], w_cv[0],
                 w_co[0], norm_ffn_g[0], w_pq[0], sub_keys[0], peer_u[0], peer_v[0], final_norm_g)
    return out.reshape(batch, seq, d)
```

```python
import functools
import math

import jax
import jax.numpy as jnp
import numpy as np
from jax import lax
from jax.experimental import pallas as pl
from jax.experimental.pallas import tpu as pltpu

F32 = jnp.float32
BF16 = jnp.bfloat16
EPS = 1e-6

ML_HEADS = 4
ML_HEAD_DIM = 128
ML_CHUNK = 64
CONV_WIDTH = 4
DA_HEADS = 4
DA_QK_DIM = 64
DA_VDIM = 128
CA_HEADS = 4
PEER_HEADS = 8
PEER_KEYS = 128
PEER_TOPK = 16
PEER_HALF = 64
LANES = 128

VMEM_LIMIT = 56 * 1024 * 1024


def _mm(a, b):
    return jnp.dot(a, b, preferred_element_type=F32)


def _mm_nt(a, b):
    return lax.dot_general(a, b, (((1,), (1,)), ((), ())), preferred_element_type=F32)


def _rms(x, g):
    return x * lax.rsqrt(jnp.mean(x * x, axis=-1, keepdims=True) + EPS) * g


def _params(sem):
    return pltpu.CompilerParams(dimension_semantics=sem, vmem_limit_bytes=VMEM_LIMIT)


def _inproj_kernel(x_ref, g_ref, w_ref, wg_ref, z_ref, gate_ref):
    hb = _rms(x_ref[...], g_ref[...]).astype(BF16)
    z_ref[...] = _mm(hb, w_ref[...]).astype(BF16)
    gate_ref[...] = _mm(hb, wg_ref[...])


def _inproj(x2d, g, w_main, w_gate, tm):
    n, d = x2d.shape
    wz = w_main.shape[1]
    return pl.pallas_call(
        _inproj_kernel,
        grid=(n // tm,),
        in_specs=[
            pl.BlockSpec((tm, d), lambda i: (i, 0)),
            pl.BlockSpec((1, d), lambda i: (0, 0)),
            pl.BlockSpec((d, wz), lambda i: (0, 0)),
            pl.BlockSpec((d, LANES), lambda i: (0, 0)),
        ],
        out_specs=[
            pl.BlockSpec((tm, wz), lambda i: (i, 0)),
            pl.BlockSpec((tm, LANES), lambda i: (i, 0)),
        ],
        out_shape=[
            jax.ShapeDtypeStruct((n, wz), BF16),
            jax.ShapeDtypeStruct((n, LANES), F32),
        ],
        compiler_params=_params(("parallel",)),
        name="inproj",
    )(x2d, g, w_main, w_gate)


def _mlstm_kernel(zqk_ref, zv_ref, zo_ref, gates_ref, convw_ref, gbias_ref, gml_ref, out_ref,
                  qk_scr, c_scr, *, seq):
    width = ML_HEADS * ML_HEAD_DIM
    conv_rows = 128
    halo = 16

    w = convw_ref[...]
    col = lax.broadcasted_iota(jnp.int32, (1, 2 * width), 1)
    kscale = jnp.where(col >= width, ML_HEAD_DIM ** -0.5, 1.0).astype(F32)

    def conv_body(rb, _):
        r0 = pl.multiple_of(rb * conv_rows, conv_rows)
        main = zqk_ref[pl.ds(r0, conv_rows), :].astype(F32)
        prev0 = pl.multiple_of(jnp.maximum(r0 - halo, 0), halo)
        prev = zqk_ref[pl.ds(prev0, halo), :].astype(F32)
        prev = jnp.where(rb > 0, prev, 0.0)
        xx = jnp.concatenate([prev, main], axis=0)
        acc = xx[halo:] * w[CONV_WIDTH - 1:CONV_WIDTH, :]
        for j in range(CONV_WIDTH - 1):
            sh = CONV_WIDTH - 1 - j
            acc = acc + pltpu.roll(xx, sh, 0)[halo:] * w[j:j + 1, :]
        y = acc * jax.nn.sigmoid(acc) * kscale
        qk_scr[pl.ds(r0, conv_rows), :] = y.astype(BF16)
        return 0

    lax.fori_loop(0, seq // conv_rows, conv_body, 0)

    c_scr[...] = jnp.zeros_like(c_scr)
    L = ML_CHUNK
    row = lax.broadcasted_iota(jnp.int32, (L, L), 0)
    colL = lax.broadcasted_iota(jnp.int32, (L, L), 1)
    tril = (colL <= row)
    tril_f = tril.astype(F32)
    lane128 = lax.broadcasted_iota(jnp.int32, (L, LANES), 1)
    ones_col = jnp.where(lane128 == 0, 1.0, 0.0).astype(BF16)
    gbias = gbias_ref[...]
    gml = gml_ref[...]

    def chunk_body(c, m_states):
        r0 = pl.multiple_of(c * L, L)
        gb = gates_ref[pl.ds(r0, L), :] + gbias
        lf = jnp.minimum(gb, 0.0) - jnp.log1p(jnp.exp(-jnp.abs(gb)))
        p = jnp.where(lane128 < ML_HEADS, gb, lf)
        cum = jnp.dot(tril_f, p, preferred_element_type=F32, precision=lax.Precision.HIGHEST)
        q_mix = jnp.where(lane128 < ML_HEADS, p, cum)
        qt = q_mix.T
        new_states = []
        for h in range(ML_HEADS):
            m_st = m_states[h]
            ic_col = p[:, h:h + 1]
            b_col = cum[:, ML_HEADS + h:ML_HEADS + h + 1]
            ic_row = qt[h:h + 1, :]
            b_row = qt[ML_HEADS + h:ML_HEADS + h + 1, :]
            b_last = b_col[L - 1:L, :]
            log_d = jnp.where(tril, b_col - b_row + ic_row, -jnp.inf)
            m_inter = b_col + m_st
            m_t = jnp.maximum(m_inter, jnp.max(log_d, axis=-1, keepdims=True))
            dmat = jnp.exp(log_d - m_t)
            hs = slice(h * ML_HEAD_DIM, (h + 1) * ML_HEAD_DIM)
            ks = slice(width + h * ML_HEAD_DIM, width + (h + 1) * ML_HEAD_DIM)
            qc = qk_scr[pl.ds(r0, L), hs]
            kc = qk_scr[pl.ds(r0, L), ks]
            vc = zv_ref[pl.ds(r0, L), hs]
            v_aug = jnp.concatenate([vc, ones_col], axis=1)
            sc = _mm_nt(qc, kc) * dmat
            inter = jnp.exp(m_inter - m_t)
            c_old = c_scr[h]
            num_aug = _mm(sc.astype(BF16), v_aug) + inter * _mm(qc, c_old.astype(BF16))
            num = num_aug[:, :ML_HEAD_DIM]
            den = num_aug[:, ML_HEAD_DIM:ML_HEAD_DIM + 1]
            hraw = num / jnp.maximum(jnp.abs(den), jnp.exp(-m_t))
            hn = _rms(hraw, gml[:, hs])
            og = zo_ref[pl.ds(r0, L), hs].astype(F32)
            out_ref[pl.ds(r0, L), hs] = (hn * jax.nn.sigmoid(og)).astype(BF16)
            g_col = b_last - b_col + ic_col
            m_next = jnp.maximum(b_last + m_st, jnp.max(g_col, axis=0, keepdims=True))
            decay = jnp.exp(b_last + m_st - m_next)
            wgt = jnp.exp(g_col - m_next)
            kw_t = (wgt * kc.astype(F32)).T.astype(BF16)
            c_scr[h] = decay * c_old + _mm(kw_t, v_aug)
            new_states.append(m_next)
        return tuple(new_states)

    init = tuple(jnp.zeros((1, 1), F32) for _ in range(ML_HEADS))
    lax.fori_loop(0, seq // L, chunk_body, init)


def _mlstm(z, gates, conv_w, gbias, gml, batch, seq):
    n = batch * seq
    width = ML_HEADS * ML_HEAD_DIM
    return pl.pallas_call(
        functools.partial(_mlstm_kernel, seq=seq),
        grid=(batch,),
        in_specs=[
            pl.BlockSpec((seq, 2 * width), lambda b: (b, 0)),
            pl.BlockSpec((seq, width), lambda b: (b, 2)),
            pl.BlockSpec((seq, width), lambda b: (b, 3)),
            pl.BlockSpec((seq, LANES), lambda b: (b, 0)),
            pl.BlockSpec((CONV_WIDTH, 2 * width), lambda b: (0, 0)),
            pl.BlockSpec((1, LANES), lambda b: (0, 0)),
            pl.BlockSpec((1, width), lambda b: (0, 0)),
        ],
        out_specs=pl.BlockSpec((seq, width), lambda b: (b, 0)),
        out_shape=jax.ShapeDtypeStruct((n, width), BF16),
        scratch_shapes=[
            pltpu.VMEM((seq, 2 * width), BF16),
            pltpu.VMEM((ML_HEADS, ML_HEAD_DIM, 2 * ML_HEAD_DIM), F32),
        ],
        compiler_params=_params(("parallel",)),
        name="mlstm",
    )(z, z, z, gates, conv_w, gbias, gml)


def _diffattn_kernel(slopes_ref, q_ref, k_ref, v_ref, lq1_ref, lk1_ref, lq2_ref, lk2_ref, g_ref,
                     o_ref, acc_ref, m_ref, l_ref, *, tq, lam_init):
    h = pl.program_id(1)
    i = pl.program_id(2)
    slope = slopes_ref[h]
    scale = DA_QK_DIM ** -0.5
    q = (q_ref[...].astype(F32) * scale).astype(BF16)
    lane = lax.broadcasted_iota(jnp.int32, q.shape, 1)
    zero = jnp.zeros_like(q)
    q2 = jnp.concatenate([jnp.where(lane < DA_QK_DIM, q, zero), jnp.where(lane >= DA_QK_DIM, q, zero)], axis=0)
    m_ref[...] = jnp.full(m_ref.shape, -jnp.inf, F32)
    l_ref[...] = jnp.zeros(l_ref.shape, F32)
    acc_ref[...] = jnp.zeros(acc_ref.shape, F32)
    r = lax.broadcasted_iota(jnp.int32, (2 * tq, tq), 0) & (tq - 1)
    c = lax.broadcasted_iota(jnp.int32, (2 * tq, tq), 1)
    rel = (r - c).astype(F32) * slope

    def block(j, masked):
        k0 = pl.multiple_of(j * tq, tq)
        ks = k_ref[pl.ds(k0, tq), :]
        vs = v_ref[pl.ds(k0, tq), :]
        s = _mm_nt(q2, ks) - (rel + ((i - j) * tq).astype(F32) * slope)
        if masked:
            s = jnp.where(c <= r, s, -jnp.inf)
        m_old = m_ref[...]
        m_new = jnp.maximum(m_old, jnp.max(s, axis=-1, keepdims=True))
        alpha = jnp.exp(m_old - m_new)
        p = jnp.exp(s - m_new)
        l_ref[...] = alpha * l_ref[...] + jnp.sum(p, axis=-1, keepdims=True)
        acc_ref[...] = alpha * acc_ref[...] + _mm(p.astype(BF16), vs)
        m_ref[...] = m_new

    def loop_body(j, carry):
        block(j, False)
        return carry

    lax.fori_loop(0, i, loop_body, 0)
    block(i, True)

    lam = (jnp.exp(jnp.sum(lq1_ref[...] * lk1_ref[...], axis=-1, keepdims=True))
           - jnp.exp(jnp.sum(lq2_ref[...] * lk2_ref[...], axis=-1, keepdims=True)) + lam_init)
    o = acc_ref[:tq] / l_ref[:tq] - lam * (acc_ref[tq:] / l_ref[tq:])
    o_ref[...] = (_rms(o, g_ref[...]) * (1.0 - lam_init)).astype(BF16)


def _diffattn(z, slopes, lq1, lk1, lq2, lk2, gda, batch, seq, tq, lam_init):
    n = batch * seq
    nq = seq // tq
    qcol0 = (4 * ML_HEADS * ML_HEAD_DIM) // LANES
    kcol0 = qcol0 + (DA_HEADS * DA_VDIM) // LANES
    vcol0 = kcol0 + (DA_HEADS * DA_VDIM) // LANES
    lam_spec = pl.BlockSpec((1, DA_QK_DIM), lambda b, h, i: (0, 0))
    return pl.pallas_call(
        functools.partial(_diffattn_kernel, tq=tq, lam_init=lam_init),
        grid=(batch, DA_HEADS, nq),
        in_specs=[
            pl.BlockSpec(memory_space=pltpu.SMEM),
            pl.BlockSpec((tq, LANES), lambda b, h, i: (b * nq + i, qcol0 + h)),
            pl.BlockSpec((seq, LANES), lambda b, h, i: (b, kcol0 + h)),
            pl.BlockSpec((seq, LANES), lambda b, h, i: (b, vcol0 + h)),
            lam_spec, lam_spec, lam_spec, lam_spec,
            pl.BlockSpec((1, DA_VDIM), lambda b, h, i: (0, h)),
        ],
        out_specs=pl.BlockSpec((tq, DA_VDIM), lambda b, h, i: (b * nq + i, h)),
        out_shape=jax.ShapeDtypeStruct((n, DA_HEADS * DA_VDIM), BF16),
        scratch_shapes=[
            pltpu.VMEM((2 * tq, DA_VDIM), F32),
            pltpu.VMEM((2 * tq, 1), F32),
            pltpu.VMEM((2 * tq, 1), F32),
        ],
        compiler_params=_params(("parallel", "parallel", "arbitrary")),
        name="diffattn",
    )(slopes, z, z, z, lq1, lk1, lq2, lk2, gda)


def _outproj_kernel(x_ref, hml_ref, hda_ref, wo1_ref, wo2_ref, g_ref, wq_ref, x1_ref, q_ref):
    x1 = x_ref[...] + _mm(hml_ref[...], wo1_ref[...]) + _mm(hda_ref[...], wo2_ref[...])
    x1_ref[...] = x1
    q_ref[...] = _mm(_rms(x1, g_ref[...]).astype(BF16), wq_ref[...]).astype(BF16)


def _outproj(x2d, hml, hda, wo1, wo2, g, wq, tm):
    n, d = x2d.shape
    half = hml.shape[1]
    row = lambda i: (i, 0)
    fixed = lambda i: (0, 0)
    return pl.pallas_call(
        _outproj_kernel,
        grid=(n // tm,),
        in_specs=[
            pl.BlockSpec((tm, d), row),
            pl.BlockSpec((tm, half), row),
            pl.BlockSpec((tm, half), row),
            pl.BlockSpec((half, d), fixed),
            pl.BlockSpec((half, d), fixed),
            pl.BlockSpec((1, d), fixed),
            pl.BlockSpec((d, d), fixed),
        ],
        out_specs=[pl.BlockSpec((tm, d), row), pl.BlockSpec((tm, d), row)],
        out_shape=[jax.ShapeDtypeStruct((n, d), F32), jax.ShapeDtypeStruct((n, d), BF16)],
        compiler_params=_params(("parallel",)),
        name="outproj",
    )(x2d, hml, hda, wo1, wo2, g, wq)


def _memkv_kernel(m_ref, g_ref, wk_ref, wv_ref, k_ref, v_ref):
    mb = _rms(m_ref[...], g_ref[...]).astype(BF16)
    k_ref[...] = _mm(mb, wk_ref[...]).astype(BF16)
    v_ref[...] = _mm(mb, wv_ref[...]).astype(BF16)


def _memkv(mem2d, g, wk, wv, tm):
    n, d = mem2d.shape
    row = lambda i: (i, 0)
    fixed = lambda i: (0, 0)
    return pl.pallas_call(
        _memkv_kernel,
        grid=(n // tm,),
        in_specs=[pl.BlockSpec((tm, d), row), pl.BlockSpec((1, d), fixed),
                  pl.BlockSpec((d, d), fixed), pl.BlockSpec((d, d), fixed)],
        out_specs=[pl.BlockSpec((tm, d), row), pl.BlockSpec((tm, d), row)],
        out_shape=[jax.ShapeDtypeStruct((n, d), BF16), jax.ShapeDtypeStruct((n, d), BF16)],
        compiler_params=_params(("parallel",)),
        name="memkv",
    )(mem2d, g, wk, wv)


def _cross_kernel(q_ref, k_ref, v_ref, x1_ref, wco_ref, g_ref, wpq_ref, sk_ref,
                  x2_ref, xn_ref, st_ref):
    d = q_ref.shape[1]
    hd = d // CA_HEADS
    scale = hd ** -0.5
    outs = []
    for hh in range(CA_HEADS):
        cs = slice(hh * hd, (hh + 1) * hd)
        s = _mm_nt(q_ref[:, cs], k_ref[:, cs]) * scale
        s = s - jnp.max(s, axis=-1, keepdims=True)
        e = jnp.exp(s)
        p = e / jnp.sum(e, axis=-1, keepdims=True)
        outs.append(_mm(p.astype(BF16), v_ref[:, cs]).astype(BF16))
    o = jnp.concatenate(outs, axis=1)
    x2 = x1_ref[...] + _mm(o, wco_ref[...])
    x2_ref[...] = x2
    xn = _rms(x2, g_ref[...])
    xn_ref[...] = xn
    qp = _mm(xn.astype(BF16), wpq_ref[...]).astype(BF16)
    lane = lax.broadcasted_iota(jnp.int32, (qp.shape[0], LANES), 1)
    for hp in range(PEER_HEADS):
        qh = qp[:, hp * LANES:(hp + 1) * LANES]
        zero = jnp.zeros_like(qh)
        sk = sk_ref[hp]
        st_ref[2 * hp] = _mm_nt(sk, jnp.where(lane < PEER_HALF, qh, zero))
        st_ref[2 * hp + 1] = _mm_nt(sk, jnp.where(lane >= PEER_HALF, qh, zero))


def _cross(qca, kca, vca, x1, wco, g, wpq, skcat, tm, seq, mem_len):
    n, d = x1.shape
    per_b = seq // tm
    row = lambda i: (i, 0)
    fixed = lambda i: (0, 0)
    memb = lambda i: (i // per_b, 0)
    ng = 2 * PEER_HEADS
    return pl.pallas_call(
        _cross_kernel,
        grid=(n // tm,),
        in_specs=[
            pl.BlockSpec((tm, d), row),
            pl.BlockSpec((mem_len, d), memb),
            pl.BlockSpec((mem_len, d), memb),
            pl.BlockSpec((tm, d), row),
            pl.BlockSpec((d, d), fixed),
            pl.BlockSpec((1, d), fixed),
            pl.BlockSpec((d, PEER_HEADS * LANES), fixed),
            pl.BlockSpec((PEER_HEADS, PEER_KEYS, LANES), lambda i: (0, 0, 0)),
        ],
        out_specs=[
            pl.BlockSpec((tm, d), row),
            pl.BlockSpec((tm, d), row),
            pl.BlockSpec((ng, PEER_KEYS, tm), lambda i: (0, 0, i)),
        ],
        out_shape=[
            jax.ShapeDtypeStruct((n, d), F32),
            jax.ShapeDtypeStruct((n, d), F32),
            jax.ShapeDtypeStruct((ng, PEER_KEYS, n), F32),
        ],
        compiler_params=_params(("parallel",)),
        name="cross",
    )(qca, kca, vca, x1, wco, g, wpq, skcat)


def _extract_topk(x, payload):
    rows, t = x.shape
    rid = lax.broadcasted_iota(jnp.int32, (rows, t), 0)
    krow = lax.broadcasted_iota(jnp.int32, (PEER_TOPK, t), 0)
    vals = jnp.zeros((PEER_TOPK, t), F32)
    picked = jnp.zeros((PEER_TOPK, t), jnp.int32)
    for r in range(PEER_TOPK):
        m = jnp.max(x, axis=0, keepdims=True)
        idx = jnp.min(jnp.where(x == m, rid, rows), axis=0, keepdims=True)
        hit = rid == idx
        if payload is None:
            sel = idx
        else:
            sel = jnp.sum(jnp.where(hit, payload, 0), axis=0, keepdims=True)
        vals = jnp.where(krow == r, m, vals)
        picked = jnp.where(krow == r, sel, picked)
        x = jnp.where(hit, -jnp.inf, x)
    return vals, picked


def _pair_candidates(va, ia, vb, ib):
    t = va.shape[1]
    sub = 8
    row = lax.broadcasted_iota(jnp.int32, (sub, t), 0)
    vals = [va[0:1] + vb]
    ids = [ia[0:1] * PEER_KEYS + ib]
    for i in range(1, sub):
        keep = PEER_TOPK // (i + 1)
        v = va[i:i + 1] + vb[:sub]
        vals.append(v if keep >= sub else jnp.where(row < keep, v, -jnp.inf))
        ids.append(ia[i:i + 1] * PEER_KEYS + ib[:sub])
    vals.append(va[sub:] + vb[0:1])
    ids.append(ia[sub:] * PEER_KEYS + ib[0:1])
    return jnp.concatenate(vals, axis=0), jnp.concatenate(ids, axis=0)


def _route_kernel(st_ref, eidx_ref, gate_ref):
    gates = []
    eids = []
    for hp in range(PEER_HEADS):
        va, ia = _extract_topk(st_ref[2 * hp], None)
        vb, ib = _extract_topk(st_ref[2 * hp + 1], None)
        cand, cid = _pair_candidates(va, ia, vb, ib)
        bv, be = _extract_topk(cand, cid)
        e = jnp.exp(bv - bv[0:1, :])
        gates.append(e / jnp.sum(e, axis=0, keepdims=True))
        eids.append(be)
    gate_ref[...] = jnp.concatenate(gates, axis=0).T
    eidx_ref[...] = jnp.concatenate(eids, axis=0).T


def _route(st, tt):
    ng, nk, n = st.shape
    width = PEER_HEADS * PEER_TOPK
    return pl.pallas_call(
        _route_kernel,
        grid=(n // tt,),
        in_specs=[pl.BlockSpec((ng, nk, tt), lambda i: (0, 0, i))],
        out_specs=[pl.BlockSpec((tt, width), lambda i: (i, 0)),
                   pl.BlockSpec((tt, width), lambda i: (i, 0))],
        out_shape=[jax.ShapeDtypeStruct((n, width), jnp.int32),
                   jax.ShapeDtypeStruct((n, width), F32)],
        compiler_params=_params(("parallel",)),
        name="route",
    )(st)


PEER_SLOTS = 8


def _peer_kernel(eidx_ref, enext_ref, gate_ref, xn_ref, x2_ref, gfin_ref, uv_hbm, out_ref, buf, sem, acc_ref, *, tb):
    nsel = PEER_HEADS * PEER_TOPK
    d = xn_ref.shape[1]
    i = pl.program_id(0)
    last_step = pl.num_programs(0) - 1
    ngroups = tb // PEER_SLOTS

    def issue(idx_ref, t, slot):
        for k in range(nsel):
            e = idx_ref[t, k]
            pltpu.make_async_copy(uv_hbm.at[e], buf.at[slot, pl.ds(k, 1), :], sem.at[slot]).start(priority=k % 2)

    def wait_slot(slot):
        pltpu.make_async_copy(buf.at[slot], buf.at[slot], sem.at[slot]).wait()

    eye = (lax.broadcasted_iota(jnp.int32, (nsel, nsel), 0) == lax.broadcasted_iota(jnp.int32, (nsel, nsel), 1))

    def compute(t, s):
        x = xn_ref[pl.ds(t, 1), :]
        w = buf[s]
        u = lax.bitcast_convert_type(w << 16, F32)
        act = jnp.sum(u * x, axis=-1, keepdims=True)
        g_row = gate_ref[pl.ds(t, 1), :]
        g_col = jnp.sum(jnp.where(eye, g_row, 0.0), axis=-1, keepdims=True)
        coef = 0.5 * act * (1.0 + lax.erf(act * (2.0 ** -0.5))) * g_col
        v = lax.bitcast_convert_type(w & jnp.uint32(0xFFFF0000), F32)
        return jnp.sum(coef * v, axis=0, keepdims=True)

    @pl.when(i == 0)
    def _():
        for s in range(PEER_SLOTS - 1):
            issue(eidx_ref, s, s)

    def group(g, carry):
        for s in range(PEER_SLOTS):
            t = g * PEER_SLOTS + s
            wait_slot(s)
            row = compute(t, s)
            issue(eidx_ref, t + PEER_SLOTS - 1, (s - 1) % PEER_SLOTS)
            acc_ref[pl.ds(t, 1), :] = row
        return carry

    lax.fori_loop(0, ngroups - 1, group, 0)

    for s in range(PEER_SLOTS):
        t = (ngroups - 1) * PEER_SLOTS + s
        wait_slot(s)
        row = compute(t, s)
        if s == 0:
            issue(eidx_ref, t + PEER_SLOTS - 1, PEER_SLOTS - 1)
        else:
            @pl.when(i < last_step)
            def _():
                issue(enext_ref, s - 1, s - 1)
        acc_ref[pl.ds(t, 1), :] = row

    out_ref[...] = _rms(x2_ref[...] + acc_ref[...], gfin_ref[...])


def _pack_experts(peer_u, peer_v):
    def bits(a):
        return lax.bitcast_convert_type(a.astype(BF16), jnp.uint16).astype(jnp.uint32)
    packed = (bits(peer_v) << 16) | bits(peer_u)
    return packed.reshape(peer_u.shape[0], 1, peer_u.shape[1])


def _peer(eidx, gate, xn, x2, gfin, uv, tb):
    n, d = xn.shape
    nsel = PEER_HEADS * PEER_TOPK
    nsteps = n // tb
    row = lambda i: (i, 0)
    nxt = lambda i: (jnp.minimum(i + 1, nsteps - 1) * (tb // PEER_SLOTS), 0)
    return pl.pallas_call(
        functools.partial(_peer_kernel, tb=tb),
        grid=(nsteps,),
        in_specs=[
            pl.BlockSpec((tb, nsel), row, memory_space=pltpu.SMEM),
            pl.BlockSpec((PEER_SLOTS, nsel), nxt, memory_space=pltpu.SMEM),
            pl.BlockSpec((tb, nsel), row),
            pl.BlockSpec((tb, d), row),
            pl.BlockSpec((tb, d), row),
            pl.BlockSpec((1, d), lambda i: (0, 0)),
            pl.BlockSpec(memory_space=pl.ANY),
        ],
        out_specs=pl.BlockSpec((tb, d), row),
        out_shape=jax.ShapeDtypeStruct((n, d), F32),
        scratch_shapes=[
            pltpu.VMEM((PEER_SLOTS, nsel, d), jnp.uint32),
            pltpu.SemaphoreType.DMA((PEER_SLOTS,)),
            pltpu.VMEM((tb, d), F32),
        ],
        compiler_params=_params(("arbitrary",)),
        name="peer",
    )(eidx, eidx, gate, xn, x2, gfin, uv)


def _layer(x2d, mem2d, batch, seq, mem_len, layer_idx, norm_mix_g, w_in, conv_w, b_igate, b_fgate,
           ml_norm_g, lambda_q1, lambda_k1, lambda_q2, lambda_k2, da_norm_g, w_out, norm_ca_g,
           norm_mem_g, w_cq, w_ck, w_cv, w_co, norm_ffn_g, w_pq, sub_keys, peer_u, peer_v, final_g):
    n, d = x2d.shape
    mlw = ML_HEADS * ML_HEAD_DIM
    gate0 = 4 * mlw
    gate1 = gate0 + 2 * ML_HEADS
    tm = min(512, n)

    w_main = jnp.concatenate([w_in[:, :gate0], w_in[:, gate1:]], axis=1).astype(BF16)
    w_gate = jnp.pad(w_in[:, gate0:gate1], ((0, 0), (0, LANES - 2 * ML_HEADS))).astype(BF16)
    gbias = jnp.pad(jnp.concatenate([b_igate, b_fgate]), (0, LANES - 2 * ML_HEADS)).reshape(1, LANES)
    row = lambda a: a.reshape(1, -1)

    z, gates = _inproj(x2d, row(norm_mix_g), w_main, w_gate, tm)
    hml = _mlstm(z, gates, conv_w, gbias, row(ml_norm_g), batch, seq)

    lam_init = 0.8 - 0.6 * math.exp(-0.3 * layer_idx)
    slopes = jnp.asarray(np.array([2.0 ** (-8.0 * (i + 1) / DA_HEADS) for i in range(DA_HEADS)], np.float32))
    hda = _diffattn(z, slopes, row(lambda_q1), row(lambda_k1), row(lambda_q2), row(lambda_k2),
                    row(da_norm_g), batch, seq, min(512, seq), lam_init)

    wo = w_out.astype(BF16)
    x1, qca = _outproj(x2d, hml, hda, wo[:mlw], wo[mlw:], row(norm_ca_g), w_cq.astype(BF16), tm)
    kca, vca = _memkv(mem2d, row(norm_mem_g), w_ck.astype(BF16), w_cv.astype(BF16), min(512, mem2d.shape[0]))

    skcat = sub_keys.transpose(0, 2, 1, 3).reshape(PEER_HEADS, PEER_KEYS, 2 * PEER_HALF).astype(BF16)
    x2, xn, st = _cross(qca, kca, vca, x1, w_co.astype(BF16), row(norm_ffn_g), w_pq.astype(BF16), skcat,
                        min(256, seq), seq, mem_len)
    eidx, gate = _route(st, min(256, n))
    return _peer(eidx, gate, xn, x2, row(final_g), _pack_experts(peer_u, peer_v), min(128, n))


def kernel(x, mem, norm_mix_g, w_in, conv_w, b_igate, b_fgate, ml_norm_g, lambda_q1, lambda_k1, lambda_q2, lambda_k2, da_norm_g, w_out, norm_ca_g, norm_mem_g, w_cq, w_ck, w_cv, w_co, norm_ffn_g, w_pq, sub_keys, peer_u, peer_v, final_norm_g):
    batch, seq, d = x.shape
    mem_len = mem.shape[1]
    depth = w_in.shape[0]
    assert depth == 1, "final norm is fused into the last layer's PEER kernel; one layer supported"
    x2d = x.reshape(batch * seq, d)
    mem2d = mem.reshape(batch * mem_len, d)
    out = _layer(x2d, mem2d, batch, seq, mem_len, 0, norm_mix_g[0], w_in[0], conv_w[0], b_igate[0],
                 b_fgate[0], ml_norm_g[0], lambda_q1[0], lambda_k1[0], lambda_q2[0], lambda_k2[0],
                 da_norm_g[0], w_out[0], norm_ca_g[0], norm_mem_g[0], w_cq[0], w_ck[0], w_cv[0],
                 w_co[0], norm_ffn_g[0], w_pq[0], sub_keys[0], peer_u[0], peer_v[0], final_norm_g)
    return out.reshape(batch, seq, d)
```

```python
import functools
import math

import jax
import jax.numpy as jnp
import numpy as np
from jax import lax
from jax.experimental import pallas as pl
from jax.experimental.pallas import tpu as pltpu

F32 = jnp.float32
BF16 = jnp.bfloat16
EPS = 1e-6

ML_HEADS = 4
ML_HEAD_DIM = 128
ML_CHUNK = 64
CONV_WIDTH = 4
DA_HEADS = 4
DA_QK_DIM = 64
DA_VDIM = 128
CA_HEADS = 4
PEER_HEADS = 8
PEER_KEYS = 128
PEER_TOPK = 16
PEER_HALF = 64
LANES = 128

VMEM_LIMIT = 56 * 1024 * 1024


def _mm(a, b):
    return jnp.dot(a, b, preferred_element_type=F32)


def _mm_nt(a, b):
    return lax.dot_general(a, b, (((1,), (1,)), ((), ())), preferred_element_type=F32)


def _rms(x, g):
    return x * lax.rsqrt(jnp.mean(x * x, axis=-1, keepdims=True) + EPS) * g


def _params(sem):
    return pltpu.CompilerParams(dimension_semantics=sem, vmem_limit_bytes=VMEM_LIMIT)


def _inproj_kernel(x_ref, g_ref, w_ref, wg_ref, z_ref, gate_ref):
    hb = _rms(x_ref[...], g_ref[...]).astype(BF16)
    z_ref[...] = _mm(hb, w_ref[...]).astype(BF16)
    gate_ref[...] = _mm(hb, wg_ref[...])


def _inproj(x2d, g, w_main, w_gate, tm):
    n, d = x2d.shape
    wz = w_main.shape[1]
    return pl.pallas_call(
        _inproj_kernel,
        grid=(n // tm,),
        in_specs=[
            pl.BlockSpec((tm, d), lambda i: (i, 0)),
            pl.BlockSpec((1, d), lambda i: (0, 0)),
            pl.BlockSpec((d, wz), lambda i: (0, 0)),
            pl.BlockSpec((d, LANES), lambda i: (0, 0)),
        ],
        out_specs=[
            pl.BlockSpec((tm, wz), lambda i: (i, 0)),
            pl.BlockSpec((tm, LANES), lambda i: (i, 0)),
        ],
        out_shape=[
            jax.ShapeDtypeStruct((n, wz), BF16),
            jax.ShapeDtypeStruct((n, LANES), F32),
        ],
        compiler_params=_params(("parallel",)),
        name="inproj",
    )(x2d, g, w_main, w_gate)


def _mlstm_kernel(zqk_ref, zv_ref, zo_ref, gates_ref, convw_ref, gbias_ref, gml_ref, out_ref,
                  qk_scr, c_scr, *, seq):
    width = ML_HEADS * ML_HEAD_DIM
    conv_rows = 128
    halo = 16

    w = convw_ref[...]
    col = lax.broadcasted_iota(jnp.int32, (1, 2 * width), 1)
    kscale = jnp.where(col >= width, ML_HEAD_DIM ** -0.5, 1.0).astype(F32)

    def conv_body(rb, _):
        r0 = pl.multiple_of(rb * conv_rows, conv_rows)
        main = zqk_ref[pl.ds(r0, conv_rows), :].astype(F32)
        prev0 = pl.multiple_of(jnp.maximum(r0 - halo, 0), halo)
        prev = zqk_ref[pl.ds(prev0, halo), :].astype(F32)
        prev = jnp.where(rb > 0, prev, 0.0)
        xx = jnp.concatenate([prev, main], axis=0)
        acc = xx[halo:] * w[CONV_WIDTH - 1:CONV_WIDTH, :]
        for j in range(CONV_WIDTH - 1):
            sh = CONV_WIDTH - 1 - j
            acc = acc + pltpu.roll(xx, sh, 0)[halo:] * w[j:j + 1, :]
        y = acc * jax.nn.sigmoid(acc) * kscale
        qk_scr[pl.ds(r0, conv_rows), :] = y.astype(BF16)
        return 0

    lax.fori_loop(0, seq // conv_rows, conv_body, 0)

    c_scr[...] = jnp.zeros_like(c_scr)
    L = ML_CHUNK
    row = lax.broadcasted_iota(jnp.int32, (L, L), 0)
    colL = lax.broadcasted_iota(jnp.int32, (L, L), 1)
    tril = (colL <= row)
    tril_f = tril.astype(F32)
    lane128 = lax.broadcasted_iota(jnp.int32, (L, LANES), 1)
    ones_col = jnp.where(lane128 == 0, 1.0, 0.0).astype(BF16)
    gbias = gbias_ref[...]
    gml = gml_ref[...]

    def chunk_body(c, m_states):
        r0 = pl.multiple_of(c * L, L)
        gb = gates_ref[pl.ds(r0, L), :] + gbias
        lf = jnp.minimum(gb, 0.0) - jnp.log1p(jnp.exp(-jnp.abs(gb)))
        p = jnp.where(lane128 < ML_HEADS, gb, lf)
        cum = jnp.dot(tril_f, p, preferred_element_type=F32, precision=lax.Precision.HIGHEST)
        q_mix = jnp.where(lane128 < ML_HEADS, p, cum)
        qt = q_mix.T
        new_states = []
        for h in range(ML_HEADS):
            m_st = m_states[h]
            ic_col = p[:, h:h + 1]
            b_col = cum[:, ML_HEADS + h:ML_HEADS + h + 1]
            ic_row = qt[h:h + 1, :]
            b_row = qt[ML_HEADS + h:ML_HEADS + h + 1, :]
            b_last = b_col[L - 1:L, :]
            log_d = jnp.where(tril, b_col - b_row + ic_row, -jnp.inf)
            m_inter = b_col + m_st
            m_t = jnp.maximum(m_inter, jnp.max(log_d, axis=-1, keepdims=True))
            dmat = jnp.exp(log_d - m_t)
            hs = slice(h * ML_HEAD_DIM, (h + 1) * ML_HEAD_DIM)
            ks = slice(width + h * ML_HEAD_DIM, width + (h + 1) * ML_HEAD_DIM)
            qc = qk_scr[pl.ds(r0, L), hs]
            kc = qk_scr[pl.ds(r0, L), ks]
            vc = zv_ref[pl.ds(r0, L), hs]
            v_aug = jnp.concatenate([vc, ones_col], axis=1)
            sc = _mm_nt(qc, kc) * dmat
            inter = jnp.exp(m_inter - m_t)
            c_old = c_scr[h]
            num_aug = _mm(sc.astype(BF16), v_aug) + inter * _mm(qc, c_old.astype(BF16))
            num = num_aug[:, :ML_HEAD_DIM]
            den = num_aug[:, ML_HEAD_DIM:ML_HEAD_DIM + 1]
            hraw = num / jnp.maximum(jnp.abs(den), jnp.exp(-m_t))
            hn = _rms(hraw, gml[:, hs])
            og = zo_ref[pl.ds(r0, L), hs].astype(F32)
            out_ref[pl.ds(r0, L), hs] = (hn * jax.nn.sigmoid(og)).astype(BF16)
            g_col = b_last - b_col + ic_col
            m_next = jnp.maximum(b_last + m_st, jnp.max(g_col, axis=0, keepdims=True))
            decay = jnp.exp(b_last + m_st - m_next)
            wgt = jnp.exp(g_col - m_next)
            kw_t = (wgt * kc.astype(F32)).T.astype(BF16)
            c_scr[h] = decay * c_old + _mm(kw_t, v_aug)
            new_states.append(m_next)
        return tuple(new_states)

    init = tuple(jnp.zeros((1, 1), F32) for _ in range(ML_HEADS))
    lax.fori_loop(0, seq // L, chunk_body, init)


def _mlstm(z, gates, conv_w, gbias, gml, batch, seq):
    n = batch * seq
    width = ML_HEADS * ML_HEAD_DIM
    return pl.pallas_call(
        functools.partial(_mlstm_kernel, seq=seq),
        grid=(batch,),
        in_specs=[
            pl.BlockSpec((seq, 2 * width), lambda b: (b, 0)),
            pl.BlockSpec((seq, width), lambda b: (b, 2)),
            pl.BlockSpec((seq, width), lambda b: (b, 3)),
            pl.BlockSpec((seq, LANES), lambda b: (b, 0)),
            pl.BlockSpec((CONV_WIDTH, 2 * width), lambda b: (0, 0)),
            pl.BlockSpec((1, LANES), lambda b: (0, 0)),
            pl.BlockSpec((1, width), lambda b: (0, 0)),
        ],
        out_specs=pl.BlockSpec((seq, width), lambda b: (b, 0)),
        out_shape=jax.ShapeDtypeStruct((n, width), BF16),
        scratch_shapes=[
            pltpu.VMEM((seq, 2 * width), BF16),
            pltpu.VMEM((ML_HEADS, ML_HEAD_DIM, 2 * ML_HEAD_DIM), F32),
        ],
        compiler_params=_params(("parallel",)),
        name="mlstm",
    )(z, z, z, gates, conv_w, gbias, gml)


def _diffattn_kernel(slopes_ref, q_ref, k_ref, v_ref, lq1_ref, lk1_ref, lq2_ref, lk2_ref, g_ref,
                     o_ref, vt_ref, acc_ref, m_ref, l_ref, *, tq, tk, lam_init):
    h = pl.program_id(1)
    i = pl.program_id(2)
    slope = slopes_ref[h]
    scale = DA_QK_DIM ** -0.5

    @pl.when(i == 0)
    def _():
        vt_ref[...] = v_ref[...].astype(F32).T.astype(BF16)

    q = (q_ref[...].astype(F32) * scale).astype(BF16)
    lane = lax.broadcasted_iota(jnp.int32, q.shape, 1)
    zero = jnp.zeros_like(q)
    q2 = jnp.concatenate([jnp.where(lane < DA_QK_DIM, q, zero), jnp.where(lane >= DA_QK_DIM, q, zero)], axis=0)
    m_ref[...] = jnp.full(m_ref.shape, -jnp.inf, F32)
    l_ref[...] = jnp.zeros(l_ref.shape, F32)
    acc_ref[...] = jnp.zeros(acc_ref.shape, F32)
    kk = lax.broadcasted_iota(jnp.int32, (tk, 2 * tq), 0)
    qq = lax.broadcasted_iota(jnp.int32, (tk, 2 * tq), 1) & (tq - 1)
    kmq = kk - qq
    rel = kmq.astype(F32) * (-slope)

    def block(j, masked):
        k0 = pl.multiple_of(j * tk, tk)
        ks = k_ref[pl.ds(k0, tk), :]
        delta = i * tq - j * tk
        off = delta.astype(F32) * slope
        s = _mm_nt(ks, q2) - rel
        if masked:
            s = jnp.where(kmq <= delta, s, -jnp.inf)
        m_old = m_ref[...]
        m_new = jnp.maximum(m_old, jnp.max(s, axis=0, keepdims=True) - off)
        alpha = jnp.exp(m_old - m_new)
        p = jnp.exp(s - (m_new + off))
        l_ref[...] = alpha * l_ref[...] + jnp.sum(p, axis=0, keepdims=True)
        acc_ref[...] = alpha * acc_ref[...] + _mm(vt_ref[:, pl.ds(k0, tk)], p.astype(BF16))
        m_ref[...] = m_new

    def full_body(j, carry):
        block(j, False)
        return carry

    def diag_body(j, carry):
        block(j, True)
        return carry

    per_q = tq // tk
    lax.fori_loop(0, i * per_q, full_body, 0)
    lax.fori_loop(i * per_q, (i + 1) * per_q, diag_body, 0)

    lam = (jnp.exp(jnp.sum(lq1_ref[...] * lk1_ref[...], axis=-1, keepdims=True))
           - jnp.exp(jnp.sum(lq2_ref[...] * lk2_ref[...], axis=-1, keepdims=True)) + lam_init)
    ot = acc_ref[:, :tq] / l_ref[:, :tq] - lam * (acc_ref[:, tq:] / l_ref[:, tq:])
    o_ref[...] = (_rms(ot.T, g_ref[...]) * (1.0 - lam_init)).astype(BF16)


def _diffattn(z, slopes, lq1, lk1, lq2, lk2, gda, batch, seq, tq, tk, lam_init):
    n = batch * seq
    nq = seq // tq
    qcol0 = (4 * ML_HEADS * ML_HEAD_DIM) // LANES
    kcol0 = qcol0 + (DA_HEADS * DA_VDIM) // LANES
    vcol0 = kcol0 + (DA_HEADS * DA_VDIM) // LANES
    lam_spec = pl.BlockSpec((1, DA_QK_DIM), lambda b, h, i: (0, 0))
    return pl.pallas_call(
        functools.partial(_diffattn_kernel, tq=tq, tk=tk, lam_init=lam_init),
        grid=(batch, DA_HEADS, nq),
        in_specs=[
            pl.BlockSpec(memory_space=pltpu.SMEM),
            pl.BlockSpec((tq, LANES), lambda b, h, i: (b * nq + i, qcol0 + h)),
            pl.BlockSpec((seq, LANES), lambda b, h, i: (b, kcol0 + h)),
            pl.BlockSpec((seq, LANES), lambda b, h, i: (b, vcol0 + h)),
            lam_spec, lam_spec, lam_spec, lam_spec,
            pl.BlockSpec((1, DA_VDIM), lambda b, h, i: (0, h)),
        ],
        out_specs=pl.BlockSpec((tq, DA_VDIM), lambda b, h, i: (b * nq + i, h)),
        out_shape=jax.ShapeDtypeStruct((n, DA_HEADS * DA_VDIM), BF16),
        scratch_shapes=[
            pltpu.VMEM((DA_VDIM, seq), BF16),
            pltpu.VMEM((DA_VDIM, 2 * tq), F32),
            pltpu.VMEM((1, 2 * tq), F32),
            pltpu.VMEM((1, 2 * tq), F32),
        ],
        compiler_params=_params(("parallel", "parallel", "arbitrary")),
        name="diffattn",
    )(slopes, z, z, z, lq1, lk1, lq2, lk2, gda)


def _outproj_kernel(x_ref, hml_ref, hda_ref, wo1_ref, wo2_ref, g_ref, wq_ref, x1_ref, q_ref):
    x1 = x_ref[...] + _mm(hml_ref[...], wo1_ref[...]) + _mm(hda_ref[...], wo2_ref[...])
    x1_ref[...] = x1
    q_ref[...] = _mm(_rms(x1, g_ref[...]).astype(BF16), wq_ref[...]).astype(BF16)


def _outproj(x2d, hml, hda, wo1, wo2, g, wq, tm):
    n, d = x2d.shape
    half = hml.shape[1]
    row = lambda i: (i, 0)
    fixed = lambda i: (0, 0)
    return pl.pallas_call(
        _outproj_kernel,
        grid=(n // tm,),
        in_specs=[
            pl.BlockSpec((tm, d), row),
            pl.BlockSpec((tm, half), row),
            pl.BlockSpec((tm, half), row),
            pl.BlockSpec((half, d), fixed),
            pl.BlockSpec((half, d), fixed),
            pl.BlockSpec((1, d), fixed),
            pl.BlockSpec((d, d), fixed),
        ],
        out_specs=[pl.BlockSpec((tm, d), row), pl.BlockSpec((tm, d), row)],
        out_shape=[jax.ShapeDtypeStruct((n, d), F32), jax.ShapeDtypeStruct((n, d), BF16)],
        compiler_params=_params(("parallel",)),
        name="outproj",
    )(x2d, hml, hda, wo1, wo2, g, wq)


def _memkv_kernel(m_ref, g_ref, wk_ref, wv_ref, k_ref, v_ref):
    mb = _rms(m_ref[...], g_ref[...]).astype(BF16)
    k_ref[...] = _mm(mb, wk_ref[...]).astype(BF16)
    v_ref[...] = _mm(mb, wv_ref[...]).astype(BF16)


def _memkv(mem2d, g, wk, wv, tm):
    n, d = mem2d.shape
    row = lambda i: (i, 0)
    fixed = lambda i: (0, 0)
    return pl.pallas_call(
        _memkv_kernel,
        grid=(n // tm,),
        in_specs=[pl.BlockSpec((tm, d), row), pl.BlockSpec((1, d), fixed),
                  pl.BlockSpec((d, d), fixed), pl.BlockSpec((d, d), fixed)],
        out_specs=[pl.BlockSpec((tm, d), row), pl.BlockSpec((tm, d), row)],
        out_shape=[jax.ShapeDtypeStruct((n, d), BF16), jax.ShapeDtypeStruct((n, d), BF16)],
        compiler_params=_params(("parallel",)),
        name="memkv",
    )(mem2d, g, wk, wv)


def _cross_kernel(q_ref, k_ref, v_ref, x1_ref, wco_ref, g_ref, wpq_ref, sk_ref,
                  x2_ref, xn_ref, st_ref):
    d = q_ref.shape[1]
    hd = d // CA_HEADS
    scale = hd ** -0.5
    outs = []
    for hh in range(CA_HEADS):
        cs = slice(hh * hd, (hh + 1) * hd)
        s = _mm_nt(q_ref[:, cs], k_ref[:, cs]) * scale
        s = s - jnp.max(s, axis=-1, keepdims=True)
        e = jnp.exp(s)
        p = e / jnp.sum(e, axis=-1, keepdims=True)
        outs.append(_mm(p.astype(BF16), v_ref[:, cs]).astype(BF16))
    o = jnp.concatenate(outs, axis=1)
    x2 = x1_ref[...] + _mm(o, wco_ref[...])
    x2_ref[...] = x2
    xn = _rms(x2, g_ref[...])
    xn_ref[...] = xn
    qp = _mm(xn.astype(BF16), wpq_ref[...]).astype(BF16)
    lane = lax.broadcasted_iota(jnp.int32, (qp.shape[0], LANES), 1)
    for hp in range(PEER_HEADS):
        qh = qp[:, hp * LANES:(hp + 1) * LANES]
        zero = jnp.zeros_like(qh)
        sk = sk_ref[hp]
        st_ref[2 * hp] = _mm_nt(sk, jnp.where(lane < PEER_HALF, qh, zero))
        st_ref[2 * hp + 1] = _mm_nt(sk, jnp.where(lane >= PEER_HALF, qh, zero))


def _cross(qca, kca, vca, x1, wco, g, wpq, skcat, tm, seq, mem_len):
    n, d = x1.shape
    per_b = seq // tm
    row = lambda i: (i, 0)
    fixed = lambda i: (0, 0)
    memb = lambda i: (i // per_b, 0)
    ng = 2 * PEER_HEADS
    return pl.pallas_call(
        _cross_kernel,
        grid=(n // tm,),
        in_specs=[
            pl.BlockSpec((tm, d), row),
            pl.BlockSpec((mem_len, d), memb),
            pl.BlockSpec((mem_len, d), memb),
            pl.BlockSpec((tm, d), row),
            pl.BlockSpec((d, d), fixed),
            pl.BlockSpec((1, d), fixed),
            pl.BlockSpec((d, PEER_HEADS * LANES), fixed),
            pl.BlockSpec((PEER_HEADS, PEER_KEYS, LANES), lambda i: (0, 0, 0)),
        ],
        out_specs=[
            pl.BlockSpec((tm, d), row),
            pl.BlockSpec((tm, d), row),
            pl.BlockSpec((ng, PEER_KEYS, tm), lambda i: (0, 0, i)),
        ],
        out_shape=[
            jax.ShapeDtypeStruct((n, d), F32),
            jax.ShapeDtypeStruct((n, d), F32),
            jax.ShapeDtypeStruct((ng, PEER_KEYS, n), F32),
        ],
        compiler_params=_params(("parallel",)),
        name="cross",
    )(qca, kca, vca, x1, wco, g, wpq, skcat)


def _extract_topk(x, payload):
    rows, t = x.shape
    rid = lax.broadcasted_iota(jnp.int32, (rows, t), 0)
    krow = lax.broadcasted_iota(jnp.int32, (PEER_TOPK, t), 0)
    vals = jnp.zeros((PEER_TOPK, t), F32)
    picked = jnp.zeros((PEER_TOPK, t), jnp.int32)
    for r in range(PEER_TOPK):
        m = jnp.max(x, axis=0, keepdims=True)
        idx = jnp.min(jnp.where(x == m, rid, rows), axis=0, keepdims=True)
        hit = rid == idx
        if payload is None:
            sel = idx
        else:
            sel = jnp.sum(jnp.where(hit, payload, 0), axis=0, keepdims=True)
        vals = jnp.where(krow == r, m, vals)
        picked = jnp.where(krow == r, sel, picked)
        x = jnp.where(hit, -jnp.inf, x)
    return vals, picked


def _pair_candidates(va, ia, vb, ib):
    t = va.shape[1]
    sub = 8
    row = lax.broadcasted_iota(jnp.int32, (sub, t), 0)
    vals = [va[0:1] + vb]
    ids = [ia[0:1] * PEER_KEYS + ib]
    for i in range(1, sub):
        keep = PEER_TOPK // (i + 1)
        v = va[i:i + 1] + vb[:sub]
        vals.append(v if keep >= sub else jnp.where(row < keep, v, -jnp.inf))
        ids.append(ia[i:i + 1] * PEER_KEYS + ib[:sub])
    vals.append(va[sub:] + vb[0:1])
    ids.append(ia[sub:] * PEER_KEYS + ib[0:1])
    return jnp.concatenate(vals, axis=0), jnp.concatenate(ids, axis=0)


def _route_kernel(st_ref, eidx_ref, gate_ref):
    gates = []
    eids = []
    for hp in range(PEER_HEADS):
        va, ia = _extract_topk(st_ref[2 * hp], None)
        vb, ib = _extract_topk(st_ref[2 * hp + 1], None)
        cand, cid = _pair_candidates(va, ia, vb, ib)
        bv, be = _extract_topk(cand, cid)
        e = jnp.exp(bv - bv[0:1, :])
        gates.append(e / jnp.sum(e, axis=0, keepdims=True))
        eids.append(be)
    gate_ref[...] = jnp.concatenate(gates, axis=0).T
    eidx_ref[...] = jnp.concatenate(eids, axis=0).T


def _route(st, tt):
    ng, nk, n = st.shape
    width = PEER_HEADS * PEER_TOPK
    return pl.pallas_call(
        _route_kernel,
        grid=(n // tt,),
        in_specs=[pl.BlockSpec((ng, nk, tt), lambda i: (0, 0, i))],
        out_specs=[pl.BlockSpec((tt, width), lambda i: (i, 0)),
                   pl.BlockSpec((tt, width), lambda i: (i, 0))],
        out_shape=[jax.ShapeDtypeStruct((n, width), jnp.int32),
                   jax.ShapeDtypeStruct((n, width), F32)],
        compiler_params=_params(("parallel",)),
        name="route",
    )(st)


PEER_SLOTS = 8
PEER_ROW_TILES = 8
PEER_PITCH = PEER_ROW_TILES + 1


def _peer_kernel(eidx_ref, enext_ref, gate_ref, xn_ref, x2_ref, gfin_ref, uv_hbm, out_ref, buf, sem, acc_ref, *, tb):
    nsel = PEER_HEADS * PEER_TOPK
    d = xn_ref.shape[1]
    i = pl.program_id(0)
    last_step = pl.num_programs(0) - 1
    ngroups = tb // PEER_SLOTS

    sub = PEER_ROW_TILES

    def issue(idx_ref, t, slot):
        for k in range(nsel):
            e8 = pl.multiple_of(idx_ref[t, k] * sub, sub)
            dst = buf.at[pl.ds((slot * nsel + k) * PEER_PITCH, sub), :]
            pltpu.make_async_copy(uv_hbm.at[pl.ds(e8, sub), :], dst, sem.at[slot]).start(priority=k % 2)

    def wait_slot(slot):
        whole = buf.at[pl.ds(0, nsel * sub), :]
        pltpu.make_async_copy(whole, whole, sem.at[slot]).wait()

    eye = (lax.broadcasted_iota(jnp.int32, (nsel, nsel), 0) == lax.broadcasted_iota(jnp.int32, (nsel, nsel), 1))

    def load_rows(s):
        groups = []
        for g in range(nsel // sub):
            base = (s * nsel + g * sub) * PEER_PITCH
            groups.append(jnp.concatenate(
                [buf[pl.ds(base + c, sub, stride=PEER_PITCH), :] for c in range(sub)], axis=1))
        return jnp.concatenate(groups, axis=0)

    def compute(t, s):
        x = xn_ref[pl.ds(t, 1), :]
        w = load_rows(s)
        u = lax.bitcast_convert_type(w << 16, F32)
        act = jnp.sum(u * x, axis=-1, keepdims=True)
        g_row = gate_ref[pl.ds(t, 1), :]
        g_col = jnp.sum(jnp.where(eye, g_row, 0.0), axis=-1, keepdims=True)
        coef = 0.5 * act * (1.0 + lax.erf(act * (2.0 ** -0.5))) * g_col
        v = lax.bitcast_convert_type(w & jnp.uint32(0xFFFF0000), F32)
        return jnp.sum(coef * v, axis=0, keepdims=True)

    @pl.when(i == 0)
    def _():
        for s in range(PEER_SLOTS - 1):
            issue(eidx_ref, s, s)

    def group(g, carry):
        for s in range(PEER_SLOTS):
            t = g * PEER_SLOTS + s
            wait_slot(s)
            row = compute(t, s)
            issue(eidx_ref, t + PEER_SLOTS - 1, (s - 1) % PEER_SLOTS)
            acc_ref[pl.ds(t, 1), :] = row
        return carry

    lax.fori_loop(0, ngroups - 1, group, 0)

    for s in range(PEER_SLOTS):
        t = (ngroups - 1) * PEER_SLOTS + s
        wait_slot(s)
        row = compute(t, s)
        if s == 0:
            issue(eidx_ref, t + PEER_SLOTS - 1, PEER_SLOTS - 1)
        else:
            @pl.when(i < last_step)
            def _():
                issue(enext_ref, s - 1, s - 1)
        acc_ref[pl.ds(t, 1), :] = row

    out_ref[...] = _rms(x2_ref[...] + acc_ref[...], gfin_ref[...])


def _pack_experts(peer_u, peer_v):
    def bits(a):
        return lax.bitcast_convert_type(a.astype(BF16), jnp.uint16).astype(jnp.uint32)
    packed = (bits(peer_v) << 16) | bits(peer_u)
    return packed.reshape(peer_u.shape[0] * PEER_ROW_TILES, LANES)


def _peer(eidx, gate, xn, x2, gfin, uv, tb):
    n, d = xn.shape
    nsel = PEER_HEADS * PEER_TOPK
    nsteps = n // tb
    row = lambda i: (i, 0)
    nxt = lambda i: (jnp.minimum(i + 1, nsteps - 1) * (tb // PEER_SLOTS), 0)
    return pl.pallas_call(
        functools.partial(_peer_kernel, tb=tb),
        grid=(nsteps,),
        in_specs=[
            pl.BlockSpec((tb, nsel), row, memory_space=pltpu.SMEM),
            pl.BlockSpec((PEER_SLOTS, nsel), nxt, memory_space=pltpu.SMEM),
            pl.BlockSpec((tb, nsel), row),
            pl.BlockSpec((tb, d), row),
            pl.BlockSpec((tb, d), row),
            pl.BlockSpec((1, d), lambda i: (0, 0)),
            pl.BlockSpec(memory_space=pl.ANY),
        ],
        out_specs=pl.BlockSpec((tb, d), row),
        out_shape=jax.ShapeDtypeStruct((n, d), F32),
        scratch_shapes=[
            pltpu.VMEM((PEER_SLOTS * nsel * PEER_PITCH, LANES), jnp.uint32),
            pltpu.SemaphoreType.DMA((PEER_SLOTS,)),
            pltpu.VMEM((tb, d), F32),
        ],
        compiler_params=_params(("arbitrary",)),
        name="peer",
    )(eidx, eidx, gate, xn, x2, gfin, uv)


def _layer(x2d, mem2d, batch, seq, mem_len, layer_idx, norm_mix_g, w_in, conv_w, b_igate, b_fgate,
           ml_norm_g, lambda_q1, lambda_k1, lambda_q2, lambda_k2, da_norm_g, w_out, norm_ca_g,
           norm_mem_g, w_cq, w_ck, w_cv, w_co, norm_ffn_g, w_pq, sub_keys, peer_u, peer_v, final_g):
    n, d = x2d.shape
    mlw = ML_HEADS * ML_HEAD_DIM
    gate0 = 4 * mlw
    gate1 = gate0 + 2 * ML_HEADS
    tm = min(512, n)

    w_main = jnp.concatenate([w_in[:, :gate0], w_in[:, gate1:]], axis=1).astype(BF16)
    w_gate = jnp.pad(w_in[:, gate0:gate1], ((0, 0), (0, LANES - 2 * ML_HEADS))).astype(BF16)
    gbias = jnp.pad(jnp.concatenate([b_igate, b_fgate]), (0, LANES - 2 * ML_HEADS)).reshape(1, LANES)
    row = lambda a: a.reshape(1, -1)

    z, gates = _inproj(x2d, row(norm_mix_g), w_main, w_gate, tm)
    hml = _mlstm(z, gates, conv_w, gbias, row(ml_norm_g), batch, seq)

    lam_init = 0.8 - 0.6 * math.exp(-0.3 * layer_idx)
    slopes = jnp.asarray(np.array([2.0 ** (-8.0 * (i + 1) / DA_HEADS) for i in range(DA_HEADS)], np.float32))
    hda = _diffattn(z, slopes, row(lambda_q1), row(lambda_k1), row(lambda_q2), row(lambda_k2),
                    row(da_norm_g), batch, seq, min(512, seq), min(256, seq), lam_init)

    wo = w_out.astype(BF16)
    x1, qca = _outproj(x2d, hml, hda, wo[:mlw], wo[mlw:], row(norm_ca_g), w_cq.astype(BF16), tm)
    kca, vca = _memkv(mem2d, row(norm_mem_g), w_ck.astype(BF16), w_cv.astype(BF16), min(512, mem2d.shape[0]))

    skcat = sub_keys.transpose(0, 2, 1, 3).reshape(PEER_HEADS, PEER_KEYS, 2 * PEER_HALF).astype(BF16)
    x2, xn, st = _cross(qca, kca, vca, x1, w_co.astype(BF16), row(norm_ffn_g), w_pq.astype(BF16), skcat,
                        min(256, seq), seq, mem_len)
    eidx, gate = _route(st, min(256, n))
    return _peer(eidx, gate, xn, x2, row(final_g), _pack_experts(peer_u, peer_v), min(128, n))


def kernel(x, mem, norm_mix_g, w_in, conv_w, b_igate, b_fgate, ml_norm_g, lambda_q1, lambda_k1, lambda_q2, lambda_k2, da_norm_g, w_out, norm_ca_g, norm_mem_g, w_cq, w_ck, w_cv, w_co, norm_ffn_g, w_pq, sub_keys, peer_u, peer_v, final_norm_g):
    batch, seq, d = x.shape
    mem_len = mem.shape[1]
    depth = w_in.shape[0]
    assert depth == 1, "final norm is fused into the last layer's PEER kernel; one layer supported"
    x2d = x.reshape(batch * seq, d)
    mem2d = mem.reshape(batch * mem_len, d)
    out = _layer(x2d, mem2d, batch, seq, mem_len, 0, norm_mix_g[0], w_in[0], conv_w[0], b_igate[0],
                 b_fgate[0], ml_norm_g[0], lambda_q1[0], lambda_k1[0], lambda_q2[0], lambda_k2[0],
                 da_norm_g[0], w_out[0], norm_ca_g[0], norm_mem_g[0], w_cq[0], w_ck[0], w_cv[0],
                 w_co[0], norm_ffn_g[0], w_pq[0], sub_keys[0], peer_u[0], peer_v[0], final_norm_g)
    return out.reshape(batch, seq, d)
```

```python
import functools
import math

import jax
import jax.numpy as jnp
import numpy as np
from jax import lax
from jax.experimental import pallas as pl
from jax.experimental.pallas import tpu as pltpu

F32 = jnp.float32
BF16 = jnp.bfloat16
EPS = 1e-6

ML_HEADS = 4
ML_HEAD_DIM = 128
ML_CHUNK = 64
CONV_WIDTH = 4
DA_HEADS = 4
DA_QK_DIM = 64
DA_VDIM = 128
CA_HEADS = 4
PEER_HEADS = 8
PEER_KEYS = 128
PEER_TOPK = 16
PEER_HALF = 64
LANES = 128

VMEM_LIMIT = 56 * 1024 * 1024


def _mm(a, b):
    return jnp.dot(a, b, preferred_element_type=F32)


def _mm_nt(a, b):
    return lax.dot_general(a, b, (((1,), (1,)), ((), ())), preferred_element_type=F32)


def _rms(x, g):
    return x * lax.rsqrt(jnp.mean(x * x, axis=-1, keepdims=True) + EPS) * g


def _params(sem):
    return pltpu.CompilerParams(dimension_semantics=sem, vmem_limit_bytes=VMEM_LIMIT)


def _inproj_kernel(x_ref, g_ref, w_ref, wg_ref, z_ref, gate_ref):
    hb = _rms(x_ref[...], g_ref[...]).astype(BF16)
    z_ref[...] = _mm(hb, w_ref[...]).astype(BF16)
    gate_ref[...] = _mm(hb, wg_ref[...])


def _inproj(x2d, g, w_main, w_gate, tm):
    n, d = x2d.shape
    wz = w_main.shape[1]
    return pl.pallas_call(
        _inproj_kernel,
        grid=(n // tm,),
        in_specs=[
            pl.BlockSpec((tm, d), lambda i: (i, 0)),
            pl.BlockSpec((1, d), lambda i: (0, 0)),
            pl.BlockSpec((d, wz), lambda i: (0, 0)),
            pl.BlockSpec((d, LANES), lambda i: (0, 0)),
        ],
        out_specs=[
            pl.BlockSpec((tm, wz), lambda i: (i, 0)),
            pl.BlockSpec((tm, LANES), lambda i: (i, 0)),
        ],
        out_shape=[
            jax.ShapeDtypeStruct((n, wz), BF16),
            jax.ShapeDtypeStruct((n, LANES), F32),
        ],
        compiler_params=_params(("parallel",)),
        name="inproj",
    )(x2d, g, w_main, w_gate)


def _mlstm_kernel(zqk_ref, zv_ref, zo_ref, gates_ref, convw_ref, gbias_ref, gml_ref, out_ref,
                  qk_scr, c_scr, *, seq):
    width = ML_HEADS * ML_HEAD_DIM
    conv_rows = 128
    halo = 16

    w = convw_ref[...]
    col = lax.broadcasted_iota(jnp.int32, (1, 2 * width), 1)
    kscale = jnp.where(col >= width, ML_HEAD_DIM ** -0.5, 1.0).astype(F32)

    def conv_body(rb, _):
        r0 = pl.multiple_of(rb * conv_rows, conv_rows)
        main = zqk_ref[pl.ds(r0, conv_rows), :].astype(F32)
        prev0 = pl.multiple_of(jnp.maximum(r0 - halo, 0), halo)
        prev = zqk_ref[pl.ds(prev0, halo), :].astype(F32)
        prev = jnp.where(rb > 0, prev, 0.0)
        xx = jnp.concatenate([prev, main], axis=0)
        acc = xx[halo:] * w[CONV_WIDTH - 1:CONV_WIDTH, :]
        for j in range(CONV_WIDTH - 1):
            sh = CONV_WIDTH - 1 - j
            acc = acc + pltpu.roll(xx, sh, 0)[halo:] * w[j:j + 1, :]
        y = acc * jax.nn.sigmoid(acc) * kscale
        qk_scr[pl.ds(r0, conv_rows), :] = y.astype(BF16)
        return 0

    lax.fori_loop(0, seq // conv_rows, conv_body, 0)

    c_scr[...] = jnp.zeros_like(c_scr)
    L = ML_CHUNK
    row = lax.broadcasted_iota(jnp.int32, (L, L), 0)
    colL = lax.broadcasted_iota(jnp.int32, (L, L), 1)
    tril = (colL <= row)
    tril_f = tril.astype(F32)
    lane128 = lax.broadcasted_iota(jnp.int32, (L, LANES), 1)
    ones_col = jnp.where(lane128 == 0, 1.0, 0.0).astype(BF16)
    gbias = gbias_ref[...]
    gml = gml_ref[...]

    def chunk_body(c, m_states):
        r0 = pl.multiple_of(c * L, L)
        gb = gates_ref[pl.ds(r0, L), :] + gbias
        lf = jnp.minimum(gb, 0.0) - jnp.log1p(jnp.exp(-jnp.abs(gb)))
        p = jnp.where(lane128 < ML_HEADS, gb, lf)
        cum = jnp.dot(tril_f, p, preferred_element_type=F32, precision=lax.Precision.HIGHEST)
        q_mix = jnp.where(lane128 < ML_HEADS, p, cum)
        qt = q_mix.T
        new_states = []
        for h in range(ML_HEADS):
            m_st = m_states[h]
            ic_col = p[:, h:h + 1]
            b_col = cum[:, ML_HEADS + h:ML_HEADS + h + 1]
            ic_row = qt[h:h + 1, :]
            b_row = qt[ML_HEADS + h:ML_HEADS + h + 1, :]
            b_last = b_col[L - 1:L, :]
            log_d = jnp.where(tril, b_col - b_row + ic_row, -jnp.inf)
            m_inter = b_col + m_st
            m_t = jnp.maximum(m_inter, jnp.max(log_d, axis=-1, keepdims=True))
            dmat = jnp.exp(log_d - m_t)
            hs = slice(h * ML_HEAD_DIM, (h + 1) * ML_HEAD_DIM)
            ks = slice(width + h * ML_HEAD_DIM, width + (h + 1) * ML_HEAD_DIM)
            qc = qk_scr[pl.ds(r0, L), hs]
            kc = qk_scr[pl.ds(r0, L), ks]
            vc = zv_ref[pl.ds(r0, L), hs]
            v_aug = jnp.concatenate([vc, ones_col], axis=1)
            sc = _mm_nt(qc, kc) * dmat
            inter = jnp.exp(m_inter - m_t)
            c_old = c_scr[h]
            num_aug = _mm(sc.astype(BF16), v_aug) + inter * _mm(qc, c_old.astype(BF16))
            num = num_aug[:, :ML_HEAD_DIM]
            den = num_aug[:, ML_HEAD_DIM:ML_HEAD_DIM + 1]
            hraw = num / jnp.maximum(jnp.abs(den), jnp.exp(-m_t))
            hn = _rms(hraw, gml[:, hs])
            og = zo_ref[pl.ds(r0, L), hs].astype(F32)
            out_ref[pl.ds(r0, L), hs] = (hn * jax.nn.sigmoid(og)).astype(BF16)
            g_col = b_last - b_col + ic_col
            m_next = jnp.maximum(b_last + m_st, jnp.max(g_col, axis=0, keepdims=True))
            decay = jnp.exp(b_last + m_st - m_next)
            wgt = jnp.exp(g_col - m_next)
            kw_t = (wgt * kc.astype(F32)).T.astype(BF16)
            c_scr[h] = decay * c_old + _mm(kw_t, v_aug)
            new_states.append(m_next)
        return tuple(new_states)

    init = tuple(jnp.zeros((1, 1), F32) for _ in range(ML_HEADS))
    lax.fori_loop(0, seq // L, chunk_body, init)


def _mlstm(z, gates, conv_w, gbias, gml, batch, seq):
    n = batch * seq
    width = ML_HEADS * ML_HEAD_DIM
    return pl.pallas_call(
        functools.partial(_mlstm_kernel, seq=seq),
        grid=(batch,),
        in_specs=[
            pl.BlockSpec((seq, 2 * width), lambda b: (b, 0)),
            pl.BlockSpec((seq, width), lambda b: (b, 2)),
            pl.BlockSpec((seq, width), lambda b: (b, 3)),
            pl.BlockSpec((seq, LANES), lambda b: (b, 0)),
            pl.BlockSpec((CONV_WIDTH, 2 * width), lambda b: (0, 0)),
            pl.BlockSpec((1, LANES), lambda b: (0, 0)),
            pl.BlockSpec((1, width), lambda b: (0, 0)),
        ],
        out_specs=pl.BlockSpec((seq, width), lambda b: (b, 0)),
        out_shape=jax.ShapeDtypeStruct((n, width), BF16),
        scratch_shapes=[
            pltpu.VMEM((seq, 2 * width), BF16),
            pltpu.VMEM((ML_HEADS, ML_HEAD_DIM, 2 * ML_HEAD_DIM), F32),
        ],
        compiler_params=_params(("parallel",)),
        name="mlstm",
    )(z, z, z, gates, conv_w, gbias, gml)


def _diffattn_kernel(slopes_ref, q_ref, k_ref, v_ref, lq1_ref, lk1_ref, lq2_ref, lk2_ref, g_ref,
                     o_ref, vt_ref, acc_ref, m_ref, l_ref, *, tq, tk, lam_init):
    h = pl.program_id(1)
    i = pl.program_id(2)
    slope = slopes_ref[h]
    scale = DA_QK_DIM ** -0.5

    @pl.when(i == 0)
    def _():
        vt_ref[...] = v_ref[...].astype(F32).T.astype(BF16)

    q = (q_ref[...].astype(F32) * scale).astype(BF16)
    lane = lax.broadcasted_iota(jnp.int32, q.shape, 1)
    zero = jnp.zeros_like(q)
    q2 = jnp.concatenate([jnp.where(lane < DA_QK_DIM, q, zero), jnp.where(lane >= DA_QK_DIM, q, zero)], axis=0)
    m_ref[...] = jnp.full(m_ref.shape, -jnp.inf, F32)
    l_ref[...] = jnp.zeros(l_ref.shape, F32)
    acc_ref[...] = jnp.zeros(acc_ref.shape, F32)
    kk = lax.broadcasted_iota(jnp.int32, (tk, 2 * tq), 0)
    qq = lax.broadcasted_iota(jnp.int32, (tk, 2 * tq), 1) & (tq - 1)
    kmq = kk - qq
    rel = kmq.astype(F32) * (-slope)

    def block(j, masked):
        k0 = pl.multiple_of(j * tk, tk)
        ks = k_ref[pl.ds(k0, tk), :]
        delta = i * tq - j * tk
        off = delta.astype(F32) * slope
        s = _mm_nt(ks, q2) - rel
        if masked:
            s = jnp.where(kmq <= delta, s, -jnp.inf)
        m_old = m_ref[...]
        m_new = jnp.maximum(m_old, jnp.max(s, axis=0, keepdims=True) - off)
        alpha = jnp.exp(m_old - m_new)
        p = jnp.exp(s - (m_new + off))
        l_ref[...] = alpha * l_ref[...] + jnp.sum(p, axis=0, keepdims=True)
        acc_ref[...] = alpha * acc_ref[...] + _mm(vt_ref[:, pl.ds(k0, tk)], p.astype(BF16))
        m_ref[...] = m_new

    def full_body(j, carry):
        block(j, False)
        return carry

    def diag_body(j, carry):
        block(j, True)
        return carry

    per_q = tq // tk
    lax.fori_loop(0, i * per_q, full_body, 0)
    lax.fori_loop(i * per_q, (i + 1) * per_q, diag_body, 0)

    lam = (jnp.exp(jnp.sum(lq1_ref[...] * lk1_ref[...], axis=-1, keepdims=True))
           - jnp.exp(jnp.sum(lq2_ref[...] * lk2_ref[...], axis=-1, keepdims=True)) + lam_init)
    ot = acc_ref[:, :tq] / l_ref[:, :tq] - lam * (acc_ref[:, tq:] / l_ref[:, tq:])
    o_ref[...] = (_rms(ot.T, g_ref[...]) * (1.0 - lam_init)).astype(BF16)


def _diffattn(z, slopes, lq1, lk1, lq2, lk2, gda, batch, seq, tq, tk, lam_init):
    n = batch * seq
    nq = seq // tq
    qcol0 = (4 * ML_HEADS * ML_HEAD_DIM) // LANES
    kcol0 = qcol0 + (DA_HEADS * DA_VDIM) // LANES
    vcol0 = kcol0 + (DA_HEADS * DA_VDIM) // LANES
    lam_spec = pl.BlockSpec((1, DA_QK_DIM), lambda b, h, i: (0, 0))
    return pl.pallas_call(
        functools.partial(_diffattn_kernel, tq=tq, tk=tk, lam_init=lam_init),
        grid=(batch, DA_HEADS, nq),
        in_specs=[
            pl.BlockSpec(memory_space=pltpu.SMEM),
            pl.BlockSpec((tq, LANES), lambda b, h, i: (b * nq + i, qcol0 + h)),
            pl.BlockSpec((seq, LANES), lambda b, h, i: (b, kcol0 + h)),
            pl.BlockSpec((seq, LANES), lambda b, h, i: (b, vcol0 + h)),
            lam_spec, lam_spec, lam_spec, lam_spec,
            pl.BlockSpec((1, DA_VDIM), lambda b, h, i: (0, h)),
        ],
        out_specs=pl.BlockSpec((tq, DA_VDIM), lambda b, h, i: (b * nq + i, h)),
        out_shape=jax.ShapeDtypeStruct((n, DA_HEADS * DA_VDIM), BF16),
        scratch_shapes=[
            pltpu.VMEM((DA_VDIM, seq), BF16),
            pltpu.VMEM((DA_VDIM, 2 * tq), F32),
            pltpu.VMEM((1, 2 * tq), F32),
            pltpu.VMEM((1, 2 * tq), F32),
        ],
        compiler_params=_params(("parallel", "parallel", "arbitrary")),
        name="diffattn",
    )(slopes, z, z, z, lq1, lk1, lq2, lk2, gda)


def _outproj_kernel(x_ref, hml_ref, hda_ref, wo1_ref, wo2_ref, g_ref, wq_ref, x1_ref, q_ref):
    x1 = x_ref[...] + _mm(hml_ref[...], wo1_ref[...]) + _mm(hda_ref[...], wo2_ref[...])
    x1_ref[...] = x1
    q_ref[...] = _mm(_rms(x1, g_ref[...]).astype(BF16), wq_ref[...]).astype(BF16)


def _outproj(x2d, hml, hda, wo1, wo2, g, wq, tm):
    n, d = x2d.shape
    half = hml.shape[1]
    row = lambda i: (i, 0)
    fixed = lambda i: (0, 0)
    return pl.pallas_call(
        _outproj_kernel,
        grid=(n // tm,),
        in_specs=[
            pl.BlockSpec((tm, d), row),
            pl.BlockSpec((tm, half), row),
            pl.BlockSpec((tm, half), row),
            pl.BlockSpec((half, d), fixed),
            pl.BlockSpec((half, d), fixed),
            pl.BlockSpec((1, d), fixed),
            pl.BlockSpec((d, d), fixed),
        ],
        out_specs=[pl.BlockSpec((tm, d), row), pl.BlockSpec((tm, d), row)],
        out_shape=[jax.ShapeDtypeStruct((n, d), F32), jax.ShapeDtypeStruct((n, d), BF16)],
        compiler_params=_params(("parallel",)),
        name="outproj",
    )(x2d, hml, hda, wo1, wo2, g, wq)


def _memkv_kernel(m_ref, g_ref, wk_ref, wv_ref, k_ref, v_ref):
    mb = _rms(m_ref[...], g_ref[...]).astype(BF16)
    k_ref[...] = _mm(mb, wk_ref[...]).astype(BF16)
    v_ref[...] = _mm(mb, wv_ref[...]).astype(BF16)


def _memkv(mem2d, g, wk, wv, tm):
    n, d = mem2d.shape
    row = lambda i: (i, 0)
    fixed = lambda i: (0, 0)
    return pl.pallas_call(
        _memkv_kernel,
        grid=(n // tm,),
        in_specs=[pl.BlockSpec((tm, d), row), pl.BlockSpec((1, d), fixed),
                  pl.BlockSpec((d, d), fixed), pl.BlockSpec((d, d), fixed)],
        out_specs=[pl.BlockSpec((tm, d), row), pl.BlockSpec((tm, d), row)],
        out_shape=[jax.ShapeDtypeStruct((n, d), BF16), jax.ShapeDtypeStruct((n, d), BF16)],
        compiler_params=_params(("parallel",)),
        name="memkv",
    )(mem2d, g, wk, wv)


def _cross_kernel(q_ref, k_ref, v_ref, x1_ref, wco_ref, g_ref, wpq_ref, sk_ref,
                  x2_ref, xn_ref, st_ref):
    d = q_ref.shape[1]
    hd = d // CA_HEADS
    scale = hd ** -0.5
    outs = []
    for hh in range(CA_HEADS):
        cs = slice(hh * hd, (hh + 1) * hd)
        s = _mm_nt(q_ref[:, cs], k_ref[:, cs]) * scale
        s = s - jnp.max(s, axis=-1, keepdims=True)
        e = jnp.exp(s)
        p = e / jnp.sum(e, axis=-1, keepdims=True)
        outs.append(_mm(p.astype(BF16), v_ref[:, cs]).astype(BF16))
    o = jnp.concatenate(outs, axis=1)
    x2 = x1_ref[...] + _mm(o, wco_ref[...])
    x2_ref[...] = x2
    xn = _rms(x2, g_ref[...])
    xn_ref[...] = xn
    qp = _mm(xn.astype(BF16), wpq_ref[...]).astype(BF16)
    lane = lax.broadcasted_iota(jnp.int32, (qp.shape[0], LANES), 1)
    for hp in range(PEER_HEADS):
        qh = qp[:, hp * LANES:(hp + 1) * LANES]
        zero = jnp.zeros_like(qh)
        sk = sk_ref[hp]
        st_ref[2 * hp] = _mm_nt(sk, jnp.where(lane < PEER_HALF, qh, zero))
        st_ref[2 * hp + 1] = _mm_nt(sk, jnp.where(lane >= PEER_HALF, qh, zero))


def _cross(qca, kca, vca, x1, wco, g, wpq, skcat, tm, seq, mem_len):
    n, d = x1.shape
    per_b = seq // tm
    row = lambda i: (i, 0)
    fixed = lambda i: (0, 0)
    memb = lambda i: (i // per_b, 0)
    ng = 2 * PEER_HEADS
    return pl.pallas_call(
        _cross_kernel,
        grid=(n // tm,),
        in_specs=[
            pl.BlockSpec((tm, d), row),
            pl.BlockSpec((mem_len, d), memb),
            pl.BlockSpec((mem_len, d), memb),
            pl.BlockSpec((tm, d), row),
            pl.BlockSpec((d, d), fixed),
            pl.BlockSpec((1, d), fixed),
            pl.BlockSpec((d, PEER_HEADS * LANES), fixed),
            pl.BlockSpec((PEER_HEADS, PEER_KEYS, LANES), lambda i: (0, 0, 0)),
        ],
        out_specs=[
            pl.BlockSpec((tm, d), row),
            pl.BlockSpec((tm, d), row),
            pl.BlockSpec((ng, PEER_KEYS, tm), lambda i: (0, 0, i)),
        ],
        out_shape=[
            jax.ShapeDtypeStruct((n, d), F32),
            jax.ShapeDtypeStruct((n, d), F32),
            jax.ShapeDtypeStruct((ng, PEER_KEYS, n), F32),
        ],
        compiler_params=_params(("parallel",)),
        name="cross",
    )(qca, kca, vca, x1, wco, g, wpq, skcat)


def _extract_topk(x, payload):
    rows, t = x.shape
    rid = lax.broadcasted_iota(jnp.int32, (rows, t), 0)
    krow = lax.broadcasted_iota(jnp.int32, (PEER_TOPK, t), 0)
    vals = jnp.zeros((PEER_TOPK, t), F32)
    picked = jnp.zeros((PEER_TOPK, t), jnp.int32)
    for r in range(PEER_TOPK):
        m = jnp.max(x, axis=0, keepdims=True)
        idx = jnp.min(jnp.where(x == m, rid, rows), axis=0, keepdims=True)
        hit = rid == idx
        if payload is None:
            sel = idx
        else:
            sel = jnp.sum(jnp.where(hit, payload, 0), axis=0, keepdims=True)
        vals = jnp.where(krow == r, m, vals)
        picked = jnp.where(krow == r, sel, picked)
        x = jnp.where(hit, -jnp.inf, x)
    return vals, picked


def _pair_candidates(va, ia, vb, ib):
    t = va.shape[1]
    sub = 8
    row = lax.broadcasted_iota(jnp.int32, (sub, t), 0)
    vals = [va[0:1] + vb]
    ids = [ia[0:1] * PEER_KEYS + ib]
    for i in range(1, sub):
        keep = PEER_TOPK // (i + 1)
        v = va[i:i + 1] + vb[:sub]
        vals.append(v if keep >= sub else jnp.where(row < keep, v, -jnp.inf))
        ids.append(ia[i:i + 1] * PEER_KEYS + ib[:sub])
    vals.append(va[sub:] + vb[0:1])
    ids.append(ia[sub:] * PEER_KEYS + ib[0:1])
    return jnp.concatenate(vals, axis=0), jnp.concatenate(ids, axis=0)


def _route_kernel(st_ref, eidx_ref, gate_ref):
    gates = []
    eids = []
    for hp in range(PEER_HEADS):
        va, ia = _extract_topk(st_ref[2 * hp], None)
        vb, ib = _extract_topk(st_ref[2 * hp + 1], None)
        cand, cid = _pair_candidates(va, ia, vb, ib)
        bv, be = _extract_topk(cand, cid)
        e = jnp.exp(bv - bv[0:1, :])
        gates.append(e / jnp.sum(e, axis=0, keepdims=True))
        eids.append(be)
    gate_ref[...] = jnp.concatenate(gates, axis=0).T
    eidx_ref[...] = jnp.concatenate(eids, axis=0).T


def _route(st, tt):
    ng, nk, n = st.shape
    width = PEER_HEADS * PEER_TOPK
    return pl.pallas_call(
        _route_kernel,
        grid=(n // tt,),
        in_specs=[pl.BlockSpec((ng, nk, tt), lambda i: (0, 0, i))],
        out_specs=[pl.BlockSpec((tt, width), lambda i: (i, 0)),
                   pl.BlockSpec((tt, width), lambda i: (i, 0))],
        out_shape=[jax.ShapeDtypeStruct((n, width), jnp.int32),
                   jax.ShapeDtypeStruct((n, width), F32)],
        compiler_params=_params(("parallel",)),
        name="route",
    )(st)


PEER_SLOTS = 8
PEER_ROW_TILES = 8
PEER_PITCH = PEER_ROW_TILES + 1


def _peer_kernel(eidx_ref, gate_ref, xn_ref, x2_ref, gfin_ref, uv_hbm, out_ref, buf, sem, acc_ref, *, tb):
    nsel = PEER_HEADS * PEER_TOPK
    d = xn_ref.shape[1]
    i = pl.program_id(0)
    last_step = pl.num_programs(0) - 1
    ngroups = tb // PEER_SLOTS

    sub = PEER_ROW_TILES

    def issue(idx_ref, t, slot):
        for k in range(nsel):
            e8 = pl.multiple_of(idx_ref[t, k] * sub, sub)
            dst = buf.at[pl.ds((slot * nsel + k) * PEER_PITCH, sub), :]
            pltpu.make_async_copy(uv_hbm.at[pl.ds(e8, sub), :], dst, sem.at[slot]).start(priority=k % 2)

    def wait_slot(slot):
        whole = buf.at[pl.ds(0, nsel * sub), :]
        pltpu.make_async_copy(whole, whole, sem.at[slot]).wait()

    eye = (lax.broadcasted_iota(jnp.int32, (nsel, nsel), 0) == lax.broadcasted_iota(jnp.int32, (nsel, nsel), 1))

    def load_rows(s):
        groups = []
        for g in range(nsel // sub):
            base = (s * nsel + g * sub) * PEER_PITCH
            groups.append(jnp.concatenate(
                [buf[pl.ds(base + c, sub, stride=PEER_PITCH), :] for c in range(sub)], axis=1))
        return jnp.concatenate(groups, axis=0)

    def compute(t, s):
        x = xn_ref[pl.ds(t, 1), :]
        w = load_rows(s)
        u = lax.bitcast_convert_type(w << 16, F32)
        act = jnp.sum(u * x, axis=-1, keepdims=True)
        g_row = gate_ref[pl.ds(t, 1), :]
        g_col = jnp.sum(jnp.where(eye, g_row, 0.0), axis=-1, keepdims=True)
        coef = 0.5 * act * (1.0 + lax.erf(act * (2.0 ** -0.5))) * g_col
        v = lax.bitcast_convert_type(w & jnp.uint32(0xFFFF0000), F32)
        return jnp.sum(coef * v, axis=0, keepdims=True)

    @pl.when(i == 0)
    def _():
        for s in range(PEER_SLOTS - 1):
            issue(eidx_ref, s, s)

    def group(g, carry):
        for s in range(PEER_SLOTS):
            t = g * PEER_SLOTS + s
            wait_slot(s)
            row = compute(t, s)
            issue(eidx_ref, t + PEER_SLOTS - 1, (s - 1) % PEER_SLOTS)
            acc_ref[pl.ds(t, 1), :] = row
        return carry

    lax.fori_loop(0, ngroups, group, 0)

    @pl.when(i == last_step)
    def _():
        for s in range(PEER_SLOTS - 1):
            wait_slot(s)

    out_ref[...] = _rms(x2_ref[...] + acc_ref[...], gfin_ref[...])


def _pack_experts(peer_u, peer_v):
    def bits(a):
        return lax.bitcast_convert_type(a.astype(BF16), jnp.uint16).astype(jnp.uint32)
    packed = (bits(peer_v) << 16) | bits(peer_u)
    return packed.reshape(peer_u.shape[0] * PEER_ROW_TILES, LANES)


def _peer(eidx, gate, xn, x2, gfin, uv, tb):
    n, d = xn.shape
    nsel = PEER_HEADS * PEER_TOPK
    nsteps = n // tb
    row = lambda i: (i, 0)
    blocks = eidx.reshape(nsteps, tb, nsel)
    heads = jnp.concatenate([blocks[1:, :PEER_SLOTS], jnp.zeros((1, PEER_SLOTS, nsel), eidx.dtype)], axis=0)
    eidx_ext = jnp.concatenate([blocks, heads], axis=1)
    return pl.pallas_call(
        functools.partial(_peer_kernel, tb=tb),
        grid=(nsteps,),
        in_specs=[
            pl.BlockSpec((None, tb + PEER_SLOTS, nsel), lambda i: (i, 0, 0), memory_space=pltpu.SMEM),
            pl.BlockSpec((tb, nsel), row),
            pl.BlockSpec((tb, d), row),
            pl.BlockSpec((tb, d), row),
            pl.BlockSpec((1, d), lambda i: (0, 0)),
            pl.BlockSpec(memory_space=pl.ANY),
        ],
        out_specs=pl.BlockSpec((tb, d), row),
        out_shape=jax.ShapeDtypeStruct((n, d), F32),
        scratch_shapes=[
            pltpu.VMEM((PEER_SLOTS * nsel * PEER_PITCH, LANES), jnp.uint32),
            pltpu.SemaphoreType.DMA((PEER_SLOTS,)),
            pltpu.VMEM((tb, d), F32),
        ],
        compiler_params=_params(("arbitrary",)),
        name="peer",
    )(eidx_ext, gate, xn, x2, gfin, uv)


def _layer(x2d, mem2d, batch, seq, mem_len, layer_idx, norm_mix_g, w_in, conv_w, b_igate, b_fgate,
           ml_norm_g, lambda_q1, lambda_k1, lambda_q2, lambda_k2, da_norm_g, w_out, norm_ca_g,
           norm_mem_g, w_cq, w_ck, w_cv, w_co, norm_ffn_g, w_pq, sub_keys, peer_u, peer_v, final_g):
    n, d = x2d.shape
    mlw = ML_HEADS * ML_HEAD_DIM
    gate0 = 4 * mlw
    gate1 = gate0 + 2 * ML_HEADS
    tm = min(512, n)

    w_main = jnp.concatenate([w_in[:, :gate0], w_in[:, gate1:]], axis=1).astype(BF16)
    w_gate = jnp.pad(w_in[:, gate0:gate1], ((0, 0), (0, LANES - 2 * ML_HEADS))).astype(BF16)
    gbias = jnp.pad(jnp.concatenate([b_igate, b_fgate]), (0, LANES - 2 * ML_HEADS)).reshape(1, LANES)
    row = lambda a: a.reshape(1, -1)

    z, gates = _inproj(x2d, row(norm_mix_g), w_main, w_gate, tm)
    hml = _mlstm(z, gates, conv_w, gbias, row(ml_norm_g), batch, seq)

    lam_init = 0.8 - 0.6 * math.exp(-0.3 * layer_idx)
    slopes = jnp.asarray(np.array([2.0 ** (-8.0 * (i + 1) / DA_HEADS) for i in range(DA_HEADS)], np.float32))
    hda = _diffattn(z, slopes, row(lambda_q1), row(lambda_k1), row(lambda_q2), row(lambda_k2),
                    row(da_norm_g), batch, seq, min(512, seq), min(256, seq), lam_init)

    wo = w_out.astype(BF16)
    x1, qca = _outproj(x2d, hml, hda, wo[:mlw], wo[mlw:], row(norm_ca_g), w_cq.astype(BF16), tm)
    kca, vca = _memkv(mem2d, row(norm_mem_g), w_ck.astype(BF16), w_cv.astype(BF16), min(512, mem2d.shape[0]))

    skcat = sub_keys.transpose(0, 2, 1, 3).reshape(PEER_HEADS, PEER_KEYS, 2 * PEER_HALF).astype(BF16)
    x2, xn, st = _cross(qca, kca, vca, x1, w_co.astype(BF16), row(norm_ffn_g), w_pq.astype(BF16), skcat,
                        min(256, seq), seq, mem_len)
    eidx, gate = _route(st, min(256, n))
    return _peer(eidx, gate, xn, x2, row(final_g), _pack_experts(peer_u, peer_v), min(128, n))


def kernel(x, mem, norm_mix_g, w_in, conv_w, b_igate, b_fgate, ml_norm_g, lambda_q1, lambda_k1, lambda_q2, lambda_k2, da_norm_g, w_out, norm_ca_g, norm_mem_g, w_cq, w_ck, w_cv, w_co, norm_ffn_g, w_pq, sub_keys, peer_u, peer_v, final_norm_g):
    batch, seq, d = x.shape
    mem_len = mem.shape[1]
    depth = w_in.shape[0]
    assert depth == 1, "final norm is fused into the last layer's PEER kernel; one layer supported"
    x2d = x.reshape(batch * seq, d)
    mem2d = mem.reshape(batch * mem_len, d)
    out = _layer(x2d, mem2d, batch, seq, mem_len, 0, norm_mix_g[0], w_in[0], conv_w[0], b_igate[0],
                 b_fgate[0], ml_norm_g[0], lambda_q1[0], lambda_k1[0], lambda_q2[0], lambda_k2[0],
                 da_norm_g[0], w_out[0], norm_ca_g[0], norm_mem_g[0], w_cq[0], w_ck[0], w_cv[0],
                 w_co[0], norm_ffn_g[0], w_pq[0], sub_keys[0], peer_u[0], peer_v[0], final_norm_g)
    return out.reshape(batch, seq, d)
```

```python
import functools
import math

import jax
import jax.numpy as jnp
import numpy as np
from jax import lax
from jax.experimental import pallas as pl
from jax.experimental.pallas import tpu as pltpu

F32 = jnp.float32
BF16 = jnp.bfloat16
EPS = 1e-6

ML_HEADS = 4
ML_HEAD_DIM = 128
ML_CHUNK = 64
CONV_WIDTH = 4
DA_HEADS = 4
DA_QK_DIM = 64
DA_VDIM = 128
CA_HEADS = 4
PEER_HEADS = 8
PEER_KEYS = 128
PEER_TOPK = 16
PEER_HALF = 64
LANES = 128

VMEM_LIMIT = 56 * 1024 * 1024


def _mm(a, b):
    return jnp.dot(a, b, preferred_element_type=F32)


def _mm_nt(a, b):
    return lax.dot_general(a, b, (((1,), (1,)), ((), ())), preferred_element_type=F32)


def _rms(x, g):
    return x * lax.rsqrt(jnp.mean(x * x, axis=-1, keepdims=True) + EPS) * g


def _params(sem):
    return pltpu.CompilerParams(dimension_semantics=sem, vmem_limit_bytes=VMEM_LIMIT)


def _inproj_kernel(x_ref, g_ref, w_ref, wg_ref, z_ref, gate_ref):
    hb = _rms(x_ref[...], g_ref[...]).astype(BF16)
    z_ref[...] = _mm(hb, w_ref[...]).astype(BF16)
    gate_ref[...] = _mm(hb, wg_ref[...])


def _inproj(x2d, g, w_main, w_gate, tm):
    n, d = x2d.shape
    wz = w_main.shape[1]
    return pl.pallas_call(
        _inproj_kernel,
        grid=(n // tm,),
        in_specs=[
            pl.BlockSpec((tm, d), lambda i: (i, 0)),
            pl.BlockSpec((1, d), lambda i: (0, 0)),
            pl.BlockSpec((d, wz), lambda i: (0, 0)),
            pl.BlockSpec((d, LANES), lambda i: (0, 0)),
        ],
        out_specs=[
            pl.BlockSpec((tm, wz), lambda i: (i, 0)),
            pl.BlockSpec((tm, LANES), lambda i: (i, 0)),
        ],
        out_shape=[
            jax.ShapeDtypeStruct((n, wz), BF16),
            jax.ShapeDtypeStruct((n, LANES), F32),
        ],
        compiler_params=_params(("parallel",)),
        name="inproj",
    )(x2d, g, w_main, w_gate)


def _mlstm_kernel(zqk_ref, zv_ref, zo_ref, gates_ref, convw_ref, gbias_ref, gml_ref, out_ref,
                  qk_scr, c_scr, *, seq, nb):
    width = ML_HEADS * ML_HEAD_DIM
    conv_rows = 128
    halo = 16

    w = convw_ref[...]
    col = lax.broadcasted_iota(jnp.int32, (1, 2 * width), 1)
    kscale = jnp.where(col >= width, ML_HEAD_DIM ** -0.5, 1.0).astype(F32)

    def conv_body(rb, _):
        r0 = pl.multiple_of(rb * conv_rows, conv_rows)
        main = zqk_ref[pl.ds(r0, conv_rows), :].astype(F32)
        prev0 = pl.multiple_of(jnp.maximum(r0 - halo, 0), halo)
        prev = zqk_ref[pl.ds(prev0, halo), :].astype(F32)
        prev = jnp.where(rb % (seq // conv_rows) > 0, prev, 0.0)
        xx = jnp.concatenate([prev, main], axis=0)
        acc = xx[halo:] * w[CONV_WIDTH - 1:CONV_WIDTH, :]
        for j in range(CONV_WIDTH - 1):
            sh = CONV_WIDTH - 1 - j
            acc = acc + pltpu.roll(xx, sh, 0)[halo:] * w[j:j + 1, :]
        y = acc * jax.nn.sigmoid(acc) * kscale
        qk_scr[pl.ds(r0, conv_rows), :] = y.astype(BF16)
        return 0

    lax.fori_loop(0, nb * seq // conv_rows, conv_body, 0)

    c_scr[...] = jnp.zeros_like(c_scr)
    L = ML_CHUNK
    row = lax.broadcasted_iota(jnp.int32, (L, L), 0)
    colL = lax.broadcasted_iota(jnp.int32, (L, L), 1)
    tril = (colL <= row)
    tril_f = tril.astype(F32)
    lane128 = lax.broadcasted_iota(jnp.int32, (L, LANES), 1)
    ones_col = jnp.where(lane128 == 0, 1.0, 0.0).astype(BF16)
    gbias = gbias_ref[...]
    gml = gml_ref[...]

    def chunk_body(c, m_states):
        new_states = []
        for bb in range(nb):
            r0 = pl.multiple_of(bb * seq + c * L, L)
            gb = gates_ref[pl.ds(r0, L), :] + gbias
            lf = jnp.minimum(gb, 0.0) - jnp.log1p(jnp.exp(-jnp.abs(gb)))
            p = jnp.where(lane128 < ML_HEADS, gb, lf)
            cum = jnp.dot(tril_f, p, preferred_element_type=F32, precision=lax.Precision.HIGHEST)
            q_mix = jnp.where(lane128 < ML_HEADS, p, cum)
            qt = q_mix.T
            for h in range(ML_HEADS):
                m_st = m_states[bb * ML_HEADS + h]
                ic_col = p[:, h:h + 1]
                b_col = cum[:, ML_HEADS + h:ML_HEADS + h + 1]
                ic_row = qt[h:h + 1, :]
                b_row = qt[ML_HEADS + h:ML_HEADS + h + 1, :]
                b_last = b_col[L - 1:L, :]
                log_d = jnp.where(tril, b_col - b_row + ic_row, -jnp.inf)
                m_inter = b_col + m_st
                m_t = jnp.maximum(m_inter, jnp.max(log_d, axis=-1, keepdims=True))
                dmat = jnp.exp(log_d - m_t)
                hs = slice(h * ML_HEAD_DIM, (h + 1) * ML_HEAD_DIM)
                ks = slice(width + h * ML_HEAD_DIM, width + (h + 1) * ML_HEAD_DIM)
                qc = qk_scr[pl.ds(r0, L), hs]
                kc = qk_scr[pl.ds(r0, L), ks]
                vc = zv_ref[pl.ds(r0, L), hs]
                v_aug = jnp.concatenate([vc, ones_col], axis=1)
                sc = _mm_nt(qc, kc) * dmat
                inter = jnp.exp(m_inter - m_t)
                c_old = c_scr[bb * ML_HEADS + h]
                num_aug = _mm(sc.astype(BF16), v_aug) + inter * _mm(qc, c_old.astype(BF16))
                num = num_aug[:, :ML_HEAD_DIM]
                den = num_aug[:, ML_HEAD_DIM:ML_HEAD_DIM + 1]
                hraw = num / jnp.maximum(jnp.abs(den), jnp.exp(-m_t))
                hn = _rms(hraw, gml[:, hs])
                og = zo_ref[pl.ds(r0, L), hs].astype(F32)
                out_ref[pl.ds(r0, L), hs] = (hn * jax.nn.sigmoid(og)).astype(BF16)
                g_col = b_last - b_col + ic_col
                m_next = jnp.maximum(b_last + m_st, jnp.max(g_col, axis=0, keepdims=True))
                decay = jnp.exp(b_last + m_st - m_next)
                wgt = jnp.exp(g_col - m_next)
                kw_t = (wgt * kc.astype(F32)).T.astype(BF16)
                c_scr[bb * ML_HEADS + h] = decay * c_old + _mm(kw_t, v_aug)
                new_states.append(m_next)
        return tuple(new_states)

    init = tuple(jnp.zeros((1, 1), F32) for _ in range(nb * ML_HEADS))
    lax.fori_loop(0, seq // L, chunk_body, init)


def _mlstm(z, gates, conv_w, gbias, gml, batch, seq):
    n = batch * seq
    width = ML_HEADS * ML_HEAD_DIM
    nb = 2 if batch % 2 == 0 else 1
    rows = nb * seq
    return pl.pallas_call(
        functools.partial(_mlstm_kernel, seq=seq, nb=nb),
        grid=(batch // nb,),
        in_specs=[
            pl.BlockSpec((rows, 2 * width), lambda b: (b, 0)),
            pl.BlockSpec((rows, width), lambda b: (b, 2)),
            pl.BlockSpec((rows, width), lambda b: (b, 3)),
            pl.BlockSpec((rows, LANES), lambda b: (b, 0)),
            pl.BlockSpec((CONV_WIDTH, 2 * width), lambda b: (0, 0)),
            pl.BlockSpec((1, LANES), lambda b: (0, 0)),
            pl.BlockSpec((1, width), lambda b: (0, 0)),
        ],
        out_specs=pl.BlockSpec((rows, width), lambda b: (b, 0)),
        out_shape=jax.ShapeDtypeStruct((n, width), BF16),
        scratch_shapes=[
            pltpu.VMEM((rows, 2 * width), BF16),
            pltpu.VMEM((nb * ML_HEADS, ML_HEAD_DIM, 2 * ML_HEAD_DIM), F32),
        ],
        compiler_params=_params(("parallel",)),
        name="mlstm",
    )(z, z, z, gates, conv_w, gbias, gml)


def _diffattn_kernel(slopes_ref, q_ref, k_ref, v_ref, lq1_ref, lk1_ref, lq2_ref, lk2_ref, g_ref,
                     o_ref, vt_ref, acc_ref, m_ref, l_ref, *, tq, tk, lam_init):
    h = pl.program_id(1)
    i = pl.program_id(2)
    slope = slopes_ref[h]
    scale = DA_QK_DIM ** -0.5

    @pl.when(i == 0)
    def _():
        vt_ref[...] = v_ref[...].astype(F32).T.astype(BF16)

    q = (q_ref[...].astype(F32) * scale).astype(BF16)
    lane = lax.broadcasted_iota(jnp.int32, q.shape, 1)
    zero = jnp.zeros_like(q)
    q2 = jnp.concatenate([jnp.where(lane < DA_QK_DIM, q, zero), jnp.where(lane >= DA_QK_DIM, q, zero)], axis=0)
    m_ref[...] = jnp.full(m_ref.shape, -jnp.inf, F32)
    l_ref[...] = jnp.zeros(l_ref.shape, F32)
    acc_ref[...] = jnp.zeros(acc_ref.shape, F32)
    kk = lax.broadcasted_iota(jnp.int32, (tk, 2 * tq), 0)
    qq = lax.broadcasted_iota(jnp.int32, (tk, 2 * tq), 1) & (tq - 1)
    kmq = kk - qq
    rel = kmq.astype(F32) * (-slope)

    def block(j, masked):
        k0 = pl.multiple_of(j * tk, tk)
        ks = k_ref[pl.ds(k0, tk), :]
        delta = i * tq - j * tk
        off = delta.astype(F32) * slope
        s = _mm_nt(ks, q2) - rel
        if masked:
            s = jnp.where(kmq <= delta, s, -jnp.inf)
        m_old = m_ref[...]
        m_new = jnp.maximum(m_old, jnp.max(s, axis=0, keepdims=True) - off)
        alpha = jnp.exp(m_old - m_new)
        p = jnp.exp(s - (m_new + off))
        l_ref[...] = alpha * l_ref[...] + jnp.sum(p, axis=0, keepdims=True)
        acc_ref[...] = alpha * acc_ref[...] + _mm(vt_ref[:, pl.ds(k0, tk)], p.astype(BF16))
        m_ref[...] = m_new

    def full_body(j, carry):
        block(j, False)
        return carry

    def diag_body(j, carry):
        block(j, True)
        return carry

    per_q = tq // tk
    lax.fori_loop(0, i * per_q, full_body, 0)
    lax.fori_loop(i * per_q, (i + 1) * per_q, diag_body, 0)

    lam = (jnp.exp(jnp.sum(lq1_ref[...] * lk1_ref[...], axis=-1, keepdims=True))
           - jnp.exp(jnp.sum(lq2_ref[...] * lk2_ref[...], axis=-1, keepdims=True)) + lam_init)
    ot = acc_ref[:, :tq] / l_ref[:, :tq] - lam * (acc_ref[:, tq:] / l_ref[:, tq:])
    o_ref[...] = (_rms(ot.T, g_ref[...]) * (1.0 - lam_init)).astype(BF16)


def _diffattn(z, slopes, lq1, lk1, lq2, lk2, gda, batch, seq, tq, tk, lam_init):
    n = batch * seq
    nq = seq // tq
    qcol0 = (4 * ML_HEADS * ML_HEAD_DIM) // LANES
    kcol0 = qcol0 + (DA_HEADS * DA_VDIM) // LANES
    vcol0 = kcol0 + (DA_HEADS * DA_VDIM) // LANES
    lam_spec = pl.BlockSpec((1, DA_QK_DIM), lambda b, h, i: (0, 0))
    return pl.pallas_call(
        functools.partial(_diffattn_kernel, tq=tq, tk=tk, lam_init=lam_init),
        grid=(batch, DA_HEADS, nq),
        in_specs=[
            pl.BlockSpec(memory_space=pltpu.SMEM),
            pl.BlockSpec((tq, LANES), lambda b, h, i: (b * nq + i, qcol0 + h)),
            pl.BlockSpec((seq, LANES), lambda b, h, i: (b, kcol0 + h)),
            pl.BlockSpec((seq, LANES), lambda b, h, i: (b, vcol0 + h)),
            lam_spec, lam_spec, lam_spec, lam_spec,
            pl.BlockSpec((1, DA_VDIM), lambda b, h, i: (0, h)),
        ],
        out_specs=pl.BlockSpec((tq, DA_VDIM), lambda b, h, i: (b * nq + i, h)),
        out_shape=jax.ShapeDtypeStruct((n, DA_HEADS * DA_VDIM), BF16),
        scratch_shapes=[
            pltpu.VMEM((DA_VDIM, seq), BF16),
            pltpu.VMEM((DA_VDIM, 2 * tq), F32),
            pltpu.VMEM((1, 2 * tq), F32),
            pltpu.VMEM((1, 2 * tq), F32),
        ],
        compiler_params=_params(("parallel", "parallel", "arbitrary")),
        name="diffattn",
    )(slopes, z, z, z, lq1, lk1, lq2, lk2, gda)


def _outproj_kernel(x_ref, hml_ref, hda_ref, wo1_ref, wo2_ref, g_ref, wq_ref, x1_ref, q_ref):
    x1 = x_ref[...] + _mm(hml_ref[...], wo1_ref[...]) + _mm(hda_ref[...], wo2_ref[...])
    x1_ref[...] = x1
    q_ref[...] = _mm(_rms(x1, g_ref[...]).astype(BF16), wq_ref[...]).astype(BF16)


def _outproj(x2d, hml, hda, wo1, wo2, g, wq, tm):
    n, d = x2d.shape
    half = hml.shape[1]
    row = lambda i: (i, 0)
    fixed = lambda i: (0, 0)
    return pl.pallas_call(
        _outproj_kernel,
        grid=(n // tm,),
        in_specs=[
            pl.BlockSpec((tm, d), row),
            pl.BlockSpec((tm, half), row),
            pl.BlockSpec((tm, half), row),
            pl.BlockSpec((half, d), fixed),
            pl.BlockSpec((half, d), fixed),
            pl.BlockSpec((1, d), fixed),
            pl.BlockSpec((d, d), fixed),
        ],
        out_specs=[pl.BlockSpec((tm, d), row), pl.BlockSpec((tm, d), row)],
        out_shape=[jax.ShapeDtypeStruct((n, d), F32), jax.ShapeDtypeStruct((n, d), BF16)],
        compiler_params=_params(("parallel",)),
        name="outproj",
    )(x2d, hml, hda, wo1, wo2, g, wq)


def _memkv_kernel(m_ref, g_ref, wk_ref, wv_ref, k_ref, v_ref):
    mb = _rms(m_ref[...], g_ref[...]).astype(BF16)
    k_ref[...] = _mm(mb, wk_ref[...]).astype(BF16)
    v_ref[...] = _mm(mb, wv_ref[...]).astype(BF16)


def _memkv(mem2d, g, wk, wv, tm):
    n, d = mem2d.shape
    row = lambda i: (i, 0)
    fixed = lambda i: (0, 0)
    return pl.pallas_call(
        _memkv_kernel,
        grid=(n // tm,),
        in_specs=[pl.BlockSpec((tm, d), row), pl.BlockSpec((1, d), fixed),
                  pl.BlockSpec((d, d), fixed), pl.BlockSpec((d, d), fixed)],
        out_specs=[pl.BlockSpec((tm, d), row), pl.BlockSpec((tm, d), row)],
        out_shape=[jax.ShapeDtypeStruct((n, d), BF16), jax.ShapeDtypeStruct((n, d), BF16)],
        compiler_params=_params(("parallel",)),
        name="memkv",
    )(mem2d, g, wk, wv)


def _cross_kernel(q_ref, k_ref, v_ref, x1_ref, wco_ref, g_ref, wpq_ref, sk_ref,
                  x2_ref, eidx_ref, gate_ref):
    d = q_ref.shape[1]
    hd = d // CA_HEADS
    scale = hd ** -0.5
    outs = []
    for hh in range(CA_HEADS):
        cs = slice(hh * hd, (hh + 1) * hd)
        s = _mm_nt(q_ref[:, cs], k_ref[:, cs]) * scale
        s = s - jnp.max(s, axis=-1, keepdims=True)
        e = jnp.exp(s)
        p = e / jnp.sum(e, axis=-1, keepdims=True)
        outs.append(_mm(p.astype(BF16), v_ref[:, cs]).astype(BF16))
    o = jnp.concatenate(outs, axis=1)
    x2 = x1_ref[...] + _mm(o, wco_ref[...])
    x2_ref[...] = x2
    xn = _rms(x2, g_ref[...])
    qp = _mm(xn.astype(BF16), wpq_ref[...]).astype(BF16)
    lane = lax.broadcasted_iota(jnp.int32, (qp.shape[0], LANES), 1)
    gates = []
    eids = []
    for hp in range(PEER_HEADS):
        qh = qp[:, hp * LANES:(hp + 1) * LANES]
        zero = jnp.zeros_like(qh)
        sk = sk_ref[hp]
        g, e = _route_head(_mm_nt(sk, jnp.where(lane < PEER_HALF, qh, zero)),
                           _mm_nt(sk, jnp.where(lane >= PEER_HALF, qh, zero)))
        gates.append(g)
        eids.append(e)
    gate_ref[...] = jnp.concatenate(gates, axis=0).T
    eidx_ref[...] = jnp.concatenate(eids, axis=0).T


def _cross(qca, kca, vca, x1, wco, g, wpq, skcat, tm, seq, mem_len):
    n, d = x1.shape
    per_b = seq // tm
    row = lambda i: (i, 0)
    fixed = lambda i: (0, 0)
    memb = lambda i: (i // per_b, 0)
    width = PEER_HEADS * PEER_TOPK
    return pl.pallas_call(
        _cross_kernel,
        grid=(n // tm,),
        in_specs=[
            pl.BlockSpec((tm, d), row),
            pl.BlockSpec((mem_len, d), memb),
            pl.BlockSpec((mem_len, d), memb),
            pl.BlockSpec((tm, d), row),
            pl.BlockSpec((d, d), fixed),
            pl.BlockSpec((1, d), fixed),
            pl.BlockSpec((d, PEER_HEADS * LANES), fixed),
            pl.BlockSpec((PEER_HEADS, PEER_KEYS, LANES), lambda i: (0, 0, 0)),
        ],
        out_specs=[
            pl.BlockSpec((tm, d), row),
            pl.BlockSpec((tm, width), row),
            pl.BlockSpec((tm, width), row),
        ],
        out_shape=[
            jax.ShapeDtypeStruct((n, d), F32),
            jax.ShapeDtypeStruct((n, width), jnp.int32),
            jax.ShapeDtypeStruct((n, width), F32),
        ],
        compiler_params=_params(("parallel",)),
        name="cross",
    )(qca, kca, vca, x1, wco, g, wpq, skcat)


def _extract_topk(x, payload):
    rows, t = x.shape
    rid = lax.broadcasted_iota(jnp.int32, (rows, t), 0)
    krow = lax.broadcasted_iota(jnp.int32, (PEER_TOPK, t), 0)
    vals = jnp.zeros((PEER_TOPK, t), F32)
    picked = jnp.zeros((PEER_TOPK, t), jnp.int32)
    for r in range(PEER_TOPK):
        m = jnp.max(x, axis=0, keepdims=True)
        idx = jnp.min(jnp.where(x == m, rid, rows), axis=0, keepdims=True)
        hit = rid == idx
        if payload is None:
            sel = idx
        else:
            sel = jnp.sum(jnp.where(hit, payload, 0), axis=0, keepdims=True)
        vals = jnp.where(krow == r, m, vals)
        picked = jnp.where(krow == r, sel, picked)
        x = jnp.where(hit, -jnp.inf, x)
    return vals, picked


def _pair_candidates(va, ia, vb, ib):
    t = va.shape[1]
    sub = 8
    row = lax.broadcasted_iota(jnp.int32, (sub, t), 0)
    vals = [va[0:1] + vb]
    ids = [ia[0:1] * PEER_KEYS + ib]
    for i in range(1, sub):
        keep = PEER_TOPK // (i + 1)
        v = va[i:i + 1] + vb[:sub]
        vals.append(v if keep >= sub else jnp.where(row < keep, v, -jnp.inf))
        ids.append(ia[i:i + 1] * PEER_KEYS + ib[:sub])
    vals.append(va[sub:] + vb[0:1])
    ids.append(ia[sub:] * PEER_KEYS + ib[0:1])
    return jnp.concatenate(vals, axis=0), jnp.concatenate(ids, axis=0)


def _route_head(s0, s1):
    va, ia = _extract_topk(s0, None)
    vb, ib = _extract_topk(s1, None)
    cand, cid = _pair_candidates(va, ia, vb, ib)
    bv, be = _extract_topk(cand, cid)
    e = jnp.exp(bv - bv[0:1, :])
    return e / jnp.sum(e, axis=0, keepdims=True), be


PEER_SLOTS = 8
PEER_ROW_TILES = 8
PEER_PITCH = PEER_ROW_TILES + 1


def _peer_kernel(eidx_ref, gate_ref, x2_ref, gffn_ref, gfin_ref, uv_hbm, out_ref, buf, sem, acc_ref, xn_ref,
                 *, tb):
    nsel = PEER_HEADS * PEER_TOPK
    d = x2_ref.shape[1]
    xn_ref[...] = _rms(x2_ref[...], gffn_ref[...])
    i = pl.program_id(0)
    last_step = pl.num_programs(0) - 1
    ngroups = tb // PEER_SLOTS

    sub = PEER_ROW_TILES

    def issue(idx_ref, t, slot):
        for k in range(nsel):
            e8 = pl.multiple_of(idx_ref[t, k] * sub, sub)
            dst = buf.at[pl.ds((slot * nsel + k) * PEER_PITCH, sub), :]
            pltpu.make_async_copy(uv_hbm.at[pl.ds(e8, sub), :], dst, sem.at[slot]).start(priority=k % 2)

    def wait_slot(slot):
        whole = buf.at[pl.ds(0, nsel * sub), :]
        pltpu.make_async_copy(whole, whole, sem.at[slot]).wait()

    eye = (lax.broadcasted_iota(jnp.int32, (nsel, nsel), 0) == lax.broadcasted_iota(jnp.int32, (nsel, nsel), 1))

    def load_rows(s):
        groups = []
        for g in range(nsel // sub):
            base = (s * nsel + g * sub) * PEER_PITCH
            groups.append(jnp.concatenate(
                [buf[pl.ds(base + c, sub, stride=PEER_PITCH), :] for c in range(sub)], axis=1))
        return jnp.concatenate(groups, axis=0)

    def compute(t, s):
        x = xn_ref[pl.ds(t, 1), :]
        w = load_rows(s)
        u = lax.bitcast_convert_type(w << 16, F32)
        act = jnp.sum(u * x, axis=-1, keepdims=True)
        g_row = gate_ref[pl.ds(t, 1), :]
        g_col = jnp.sum(jnp.where(eye, g_row, 0.0), axis=-1, keepdims=True)
        coef = 0.5 * act * (1.0 + lax.erf(act * (2.0 ** -0.5))) * g_col
        v = lax.bitcast_convert_type(w & jnp.uint32(0xFFFF0000), F32)
        return jnp.sum(coef * v, axis=0, keepdims=True)

    @pl.when(i == 0)
    def _():
        for s in range(PEER_SLOTS - 1):
            issue(eidx_ref, s, s)

    def group(g, carry):
        for s in range(PEER_SLOTS):
            t = g * PEER_SLOTS + s
            wait_slot(s)
            row = compute(t, s)
            issue(eidx_ref, t + PEER_SLOTS - 1, (s - 1) % PEER_SLOTS)
            acc_ref[pl.ds(t, 1), :] = row
        return carry

    lax.fori_loop(0, ngroups, group, 0)

    @pl.when(i == last_step)
    def _():
        for s in range(PEER_SLOTS - 1):
            wait_slot(s)

    out_ref[...] = _rms(x2_ref[...] + acc_ref[...], gfin_ref[...])


def _pack_experts(peer_u, peer_v):
    def bits(a):
        return lax.bitcast_convert_type(a.astype(BF16), jnp.uint16).astype(jnp.uint32)
    packed = (bits(peer_v) << 16) | bits(peer_u)
    return packed.reshape(peer_u.shape[0] * PEER_ROW_TILES, LANES)


def _peer(eidx, gate, x2, gffn, gfin, uv, tb):
    n, d = x2.shape
    nsel = PEER_HEADS * PEER_TOPK
    nsteps = n // tb
    row = lambda i: (i, 0)
    blocks = eidx.reshape(nsteps, tb, nsel)
    heads = jnp.concatenate([blocks[1:, :PEER_SLOTS], jnp.zeros((1, PEER_SLOTS, nsel), eidx.dtype)], axis=0)
    eidx_ext = jnp.concatenate([blocks, heads], axis=1)
    return pl.pallas_call(
        functools.partial(_peer_kernel, tb=tb),
        grid=(nsteps,),
        in_specs=[
            pl.BlockSpec((None, tb + PEER_SLOTS, nsel), lambda i: (i, 0, 0), memory_space=pltpu.SMEM),
            pl.BlockSpec((tb, nsel), row),
            pl.BlockSpec((tb, d), row),
            pl.BlockSpec((1, d), lambda i: (0, 0)),
            pl.BlockSpec((1, d), lambda i: (0, 0)),
            pl.BlockSpec(memory_space=pl.ANY),
        ],
        out_specs=pl.BlockSpec((tb, d), row),
        out_shape=jax.ShapeDtypeStruct((n, d), F32),
        scratch_shapes=[
            pltpu.VMEM((PEER_SLOTS * nsel * PEER_PITCH, LANES), jnp.uint32),
            pltpu.SemaphoreType.DMA((PEER_SLOTS,)),
            pltpu.VMEM((tb, d), F32),
            pltpu.VMEM((tb, d), F32),
        ],
        compiler_params=_params(("arbitrary",)),
        name="peer",
    )(eidx_ext, gate, x2, gffn, gfin, uv)


def _layer(x2d, mem2d, batch, seq, mem_len, layer_idx, norm_mix_g, w_in, conv_w, b_igate, b_fgate,
           ml_norm_g, lambda_q1, lambda_k1, lambda_q2, lambda_k2, da_norm_g, w_out, norm_ca_g,
           norm_mem_g, w_cq, w_ck, w_cv, w_co, norm_ffn_g, w_pq, sub_keys, peer_u, peer_v, final_g):
    n, d = x2d.shape
    mlw = ML_HEADS * ML_HEAD_DIM
    gate0 = 4 * mlw
    gate1 = gate0 + 2 * ML_HEADS
    tm = min(512, n)

    w_main = jnp.concatenate([w_in[:, :gate0], w_in[:, gate1:]], axis=1).astype(BF16)
    w_gate = jnp.pad(w_in[:, gate0:gate1], ((0, 0), (0, LANES - 2 * ML_HEADS))).astype(BF16)
    gbias = jnp.pad(jnp.concatenate([b_igate, b_fgate]), (0, LANES - 2 * ML_HEADS)).reshape(1, LANES)
    row = lambda a: a.reshape(1, -1)

    z, gates = _inproj(x2d, row(norm_mix_g), w_main, w_gate, tm)
    hml = _mlstm(z, gates, conv_w, gbias, row(ml_norm_g), batch, seq)

    lam_init = 0.8 - 0.6 * math.exp(-0.3 * layer_idx)
    slopes = jnp.asarray(np.array([2.0 ** (-8.0 * (i + 1) / DA_HEADS) for i in range(DA_HEADS)], np.float32))
    hda = _diffattn(z, slopes, row(lambda_q1), row(lambda_k1), row(lambda_q2), row(lambda_k2),
                    row(da_norm_g), batch, seq, min(512, seq), min(512, seq), lam_init)

    wo = w_out.astype(BF16)
    x1, qca = _outproj(x2d, hml, hda, wo[:mlw], wo[mlw:], row(norm_ca_g), w_cq.astype(BF16), tm)
    kca, vca = _memkv(mem2d, row(norm_mem_g), w_ck.astype(BF16), w_cv.astype(BF16), min(512, mem2d.shape[0]))

    skcat = sub_keys.transpose(0, 2, 1, 3).reshape(PEER_HEADS, PEER_KEYS, 2 * PEER_HALF).astype(BF16)
    x2, eidx, gate = _cross(qca, kca, vca, x1, w_co.astype(BF16), row(norm_ffn_g), w_pq.astype(BF16), skcat,
                        min(256, seq), seq, mem_len)
    return _peer(eidx, gate, x2, row(norm_ffn_g), row(final_g), _pack_experts(peer_u, peer_v), min(128, n))


def kernel(x, mem, norm_mix_g, w_in, conv_w, b_igate, b_fgate, ml_norm_g, lambda_q1, lambda_k1, lambda_q2, lambda_k2, da_norm_g, w_out, norm_ca_g, norm_mem_g, w_cq, w_ck, w_cv, w_co, norm_ffn_g, w_pq, sub_keys, peer_u, peer_v, final_norm_g):
    batch, seq, d = x.shape
    mem_len = mem.shape[1]
    depth = w_in.shape[0]
    assert depth == 1, "final norm is fused into the last layer's PEER kernel; one layer supported"
    x2d = x.reshape(batch * seq, d)
    mem2d = mem.reshape(batch * mem_len, d)
    out = _layer(x2d, mem2d, batch, seq, mem_len, 0, norm_mix_g[0], w_in[0], conv_w[0], b_igate[0],
                 b_fgate[0], ml_norm_g[0], lambda_q1[0], lambda_k1[0], lambda_q2[0], lambda_k2[0],
                 da_norm_g[0], w_out[0], norm_ca_g[0], norm_mem_g[0], w_cq[0], w_ck[0], w_cv[0],
                 w_co[0], norm_ffn_g[0], w_pq[0], sub_keys[0], peer_u[0], peer_v[0], final_norm_g)
    return out.reshape(batch, seq, d)
```

```python
import functools
import math

import jax
import jax.numpy as jnp
import numpy as np
from jax import lax
from jax.experimental import pallas as pl
from jax.experimental.pallas import tpu as pltpu

F32 = jnp.float32
BF16 = jnp.bfloat16
EPS = 1e-6

ML_HEADS = 4
ML_HEAD_DIM = 128
ML_CHUNK = 64
CONV_WIDTH = 4
DA_HEADS = 4
DA_QK_DIM = 64
DA_VDIM = 128
CA_HEADS = 4
PEER_HEADS = 8
PEER_KEYS = 128
PEER_TOPK = 16
PEER_HALF = 64
LANES = 128

VMEM_LIMIT = 56 * 1024 * 1024


def _mm(a, b):
    return jnp.dot(a, b, preferred_element_type=F32)


def _mm_nt(a, b):
    return lax.dot_general(a, b, (((1,), (1,)), ((), ())), preferred_element_type=F32)


def _rms(x, g):
    return x * lax.rsqrt(jnp.mean(x * x, axis=-1, keepdims=True) + EPS) * g


def _params(sem):
    return pltpu.CompilerParams(dimension_semantics=sem, vmem_limit_bytes=VMEM_LIMIT)


def _inproj_kernel(x_ref, g_ref, w_ref, wg_ref, z_ref, gate_ref):
    hb = _rms(x_ref[...], g_ref[...]).astype(BF16)
    z_ref[...] = _mm(hb, w_ref[...]).astype(BF16)
    gate_ref[...] = _mm(hb, wg_ref[...])


def _inproj(x2d, g, w_main, w_gate, tm):
    n, d = x2d.shape
    wz = w_main.shape[1]
    return pl.pallas_call(
        _inproj_kernel,
        grid=(n // tm,),
        in_specs=[
            pl.BlockSpec((tm, d), lambda i: (i, 0)),
            pl.BlockSpec((1, d), lambda i: (0, 0)),
            pl.BlockSpec((d, wz), lambda i: (0, 0)),
            pl.BlockSpec((d, LANES), lambda i: (0, 0)),
        ],
        out_specs=[
            pl.BlockSpec((tm, wz), lambda i: (i, 0)),
            pl.BlockSpec((tm, LANES), lambda i: (i, 0)),
        ],
        out_shape=[
            jax.ShapeDtypeStruct((n, wz), BF16),
            jax.ShapeDtypeStruct((n, LANES), F32),
        ],
        compiler_params=_params(("parallel",)),
        name="inproj",
    )(x2d, g, w_main, w_gate)


def _mlstm_kernel(zqk_ref, zv_ref, zo_ref, gates_ref, convw_ref, gbias_ref, gml_ref, out_ref,
                  qk_scr, c_scr, *, seq, nb):
    width = ML_HEADS * ML_HEAD_DIM
    conv_rows = 128
    halo = 16

    w = convw_ref[...]
    col = lax.broadcasted_iota(jnp.int32, (1, 2 * width), 1)
    kscale = jnp.where(col >= width, ML_HEAD_DIM ** -0.5, 1.0).astype(F32)

    def conv_body(rb, _):
        r0 = pl.multiple_of(rb * conv_rows, conv_rows)
        main = zqk_ref[pl.ds(r0, conv_rows), :].astype(F32)
        prev0 = pl.multiple_of(jnp.maximum(r0 - halo, 0), halo)
        prev = zqk_ref[pl.ds(prev0, halo), :].astype(F32)
        prev = jnp.where(rb % (seq // conv_rows) > 0, prev, 0.0)
        xx = jnp.concatenate([prev, main], axis=0)
        acc = xx[halo:] * w[CONV_WIDTH - 1:CONV_WIDTH, :]
        for j in range(CONV_WIDTH - 1):
            sh = CONV_WIDTH - 1 - j
            acc = acc + pltpu.roll(xx, sh, 0)[halo:] * w[j:j + 1, :]
        y = acc * jax.nn.sigmoid(acc) * kscale
        qk_scr[pl.ds(r0, conv_rows), :] = y.astype(BF16)
        return 0

    lax.fori_loop(0, nb * seq // conv_rows, conv_body, 0)

    c_scr[...] = jnp.zeros_like(c_scr)
    L = ML_CHUNK
    row = lax.broadcasted_iota(jnp.int32, (L, L), 0)
    colL = lax.broadcasted_iota(jnp.int32, (L, L), 1)
    tril = (colL <= row)
    tril_f = tril.astype(F32)
    lane128 = lax.broadcasted_iota(jnp.int32, (L, LANES), 1)
    ones_col = jnp.where(lane128 == 0, 1.0, 0.0).astype(BF16)
    gbias = gbias_ref[...]
    gml = gml_ref[...]

    def chunk_body(c, m_states):
        new_states = []
        for bb in range(nb):
            r0 = pl.multiple_of(bb * seq + c * L, L)
            gb = gates_ref[pl.ds(r0, L), :] + gbias
            lf = jnp.minimum(gb, 0.0) - jnp.log1p(jnp.exp(-jnp.abs(gb)))
            p = jnp.where(lane128 < ML_HEADS, gb, lf)
            cum = jnp.dot(tril_f, p, preferred_element_type=F32, precision=lax.Precision.HIGHEST)
            q_mix = jnp.where(lane128 < ML_HEADS, p, cum)
            qt = q_mix.T
            for h in range(ML_HEADS):
                m_st = m_states[bb * ML_HEADS + h]
                ic_col = p[:, h:h + 1]
                b_col = cum[:, ML_HEADS + h:ML_HEADS + h + 1]
                ic_row = qt[h:h + 1, :]
                b_row = qt[ML_HEADS + h:ML_HEADS + h + 1, :]
                b_last = b_col[L - 1:L, :]
                log_d = jnp.where(tril, b_col - b_row + ic_row, -jnp.inf)
                m_inter = b_col + m_st
                m_t = jnp.maximum(m_inter, jnp.max(log_d, axis=-1, keepdims=True))
                dmat = jnp.exp(log_d - m_t)
                hs = slice(h * ML_HEAD_DIM, (h + 1) * ML_HEAD_DIM)
                ks = slice(width + h * ML_HEAD_DIM, width + (h + 1) * ML_HEAD_DIM)
                qc = qk_scr[pl.ds(r0, L), hs]
                kc = qk_scr[pl.ds(r0, L), ks]
                vc = zv_ref[pl.ds(r0, L), hs]
                v_aug = jnp.concatenate([vc, ones_col], axis=1)
                sc = _mm_nt(qc, kc) * dmat
                inter = jnp.exp(m_inter - m_t)
                c_old = c_scr[bb * ML_HEADS + h]
                num_aug = _mm(sc.astype(BF16), v_aug) + inter * _mm(qc, c_old.astype(BF16))
                num = num_aug[:, :ML_HEAD_DIM]
                den = num_aug[:, ML_HEAD_DIM:ML_HEAD_DIM + 1]
                hraw = num / jnp.maximum(jnp.abs(den), jnp.exp(-m_t))
                hn = _rms(hraw, gml[:, hs])
                og = zo_ref[pl.ds(r0, L), hs].astype(F32)
                out_ref[pl.ds(r0, L), hs] = (hn * jax.nn.sigmoid(og)).astype(BF16)
                g_col = b_last - b_col + ic_col
                m_next = jnp.maximum(b_last + m_st, jnp.max(g_col, axis=0, keepdims=True))
                decay = jnp.exp(b_last + m_st - m_next)
                wgt = jnp.exp(g_col - m_next)
                kw_t = (wgt * kc.astype(F32)).T.astype(BF16)
                c_scr[bb * ML_HEADS + h] = decay * c_old + _mm(kw_t, v_aug)
                new_states.append(m_next)
        return tuple(new_states)

    init = tuple(jnp.zeros((1, 1), F32) for _ in range(nb * ML_HEADS))
    lax.fori_loop(0, seq // L, chunk_body, init)


def _mlstm(z, gates, conv_w, gbias, gml, batch, seq):
    n = batch * seq
    width = ML_HEADS * ML_HEAD_DIM
    nb = 2 if batch % 2 == 0 else 1
    rows = nb * seq
    return pl.pallas_call(
        functools.partial(_mlstm_kernel, seq=seq, nb=nb),
        grid=(batch // nb,),
        in_specs=[
            pl.BlockSpec((rows, 2 * width), lambda b: (b, 0)),
            pl.BlockSpec((rows, width), lambda b: (b, 2)),
            pl.BlockSpec((rows, width), lambda b: (b, 3)),
            pl.BlockSpec((rows, LANES), lambda b: (b, 0)),
            pl.BlockSpec((CONV_WIDTH, 2 * width), lambda b: (0, 0)),
            pl.BlockSpec((1, LANES), lambda b: (0, 0)),
            pl.BlockSpec((1, width), lambda b: (0, 0)),
        ],
        out_specs=pl.BlockSpec((rows, width), lambda b: (b, 0)),
        out_shape=jax.ShapeDtypeStruct((n, width), BF16),
        scratch_shapes=[
            pltpu.VMEM((rows, 2 * width), BF16),
            pltpu.VMEM((nb * ML_HEADS, ML_HEAD_DIM, 2 * ML_HEAD_DIM), F32),
        ],
        compiler_params=_params(("parallel",)),
        name="mlstm",
    )(z, z, z, gates, conv_w, gbias, gml)


def _diffattn_kernel(slopes_ref, q_ref, k_ref, v_ref, lq1_ref, lk1_ref, lq2_ref, lk2_ref, g_ref,
                     o_ref, vt_ref, bias_ref, acc_ref, m_ref, l_ref, *, tq, lam_init):
    tk = tq
    h = pl.program_id(1)
    i = pl.program_id(2)
    slope = slopes_ref[h]
    scale = DA_QK_DIM ** -0.5

    @pl.when(i == 0)
    def _():
        vt_ref[...] = v_ref[...].astype(F32).T.astype(BF16)
        kk = lax.broadcasted_iota(jnp.int32, (tk, 2 * tq), 0)
        qq = lax.broadcasted_iota(jnp.int32, (tk, 2 * tq), 1) & (tq - 1)
        kmq = kk - qq
        rel = kmq.astype(F32) * (-slope)
        bias_ref[0] = rel
        bias_ref[1] = jnp.where(kmq <= 0, rel, jnp.inf)

    q = (q_ref[...].astype(F32) * scale).astype(BF16)
    lane = lax.broadcasted_iota(jnp.int32, q.shape, 1)
    zero = jnp.zeros_like(q)
    q2 = jnp.concatenate([jnp.where(lane < DA_QK_DIM, q, zero), jnp.where(lane >= DA_QK_DIM, q, zero)], axis=0)
    m_ref[...] = jnp.full(m_ref.shape, -jnp.inf, F32)
    l_ref[...] = jnp.zeros(l_ref.shape, F32)
    acc_ref[...] = jnp.zeros(acc_ref.shape, F32)

    def block(j, masked):
        k0 = pl.multiple_of(j * tk, tk)
        ks = k_ref[pl.ds(k0, tk), :]
        off = ((i - j) * tq).astype(F32) * slope
        s = _mm_nt(ks, q2) - bias_ref[1 if masked else 0]
        m_old = m_ref[...]
        m_new = jnp.maximum(m_old, jnp.max(s, axis=0, keepdims=True) - off)
        alpha = jnp.exp(m_old - m_new)
        p = jnp.exp(s - (m_new + off))
        l_ref[...] = alpha * l_ref[...] + jnp.sum(p, axis=0, keepdims=True)
        acc_ref[...] = alpha * acc_ref[...] + _mm(vt_ref[:, pl.ds(k0, tk)], p.astype(BF16))
        m_ref[...] = m_new

    def full_body(j, carry):
        block(j, False)
        return carry

    lax.fori_loop(0, i, full_body, 0)
    block(i, True)

    lam = (jnp.exp(jnp.sum(lq1_ref[...] * lk1_ref[...], axis=-1, keepdims=True))
           - jnp.exp(jnp.sum(lq2_ref[...] * lk2_ref[...], axis=-1, keepdims=True)) + lam_init)
    ot = acc_ref[:, :tq] / l_ref[:, :tq] - lam * (acc_ref[:, tq:] / l_ref[:, tq:])
    o_ref[...] = (_rms(ot.T, g_ref[...]) * (1.0 - lam_init)).astype(BF16)


def _diffattn(z, slopes, lq1, lk1, lq2, lk2, gda, batch, seq, tq, lam_init):
    n = batch * seq
    nq = seq // tq
    qcol0 = (4 * ML_HEADS * ML_HEAD_DIM) // LANES
    kcol0 = qcol0 + (DA_HEADS * DA_VDIM) // LANES
    vcol0 = kcol0 + (DA_HEADS * DA_VDIM) // LANES
    lam_spec = pl.BlockSpec((1, DA_QK_DIM), lambda b, h, i: (0, 0))
    return pl.pallas_call(
        functools.partial(_diffattn_kernel, tq=tq, lam_init=lam_init),
        grid=(batch, DA_HEADS, nq),
        in_specs=[
            pl.BlockSpec(memory_space=pltpu.SMEM),
            pl.BlockSpec((tq, LANES), lambda b, h, i: (b * nq + i, qcol0 + h)),
            pl.BlockSpec((seq, LANES), lambda b, h, i: (b, kcol0 + h)),
            pl.BlockSpec((seq, LANES), lambda b, h, i: (b, vcol0 + h)),
            lam_spec, lam_spec, lam_spec, lam_spec,
            pl.BlockSpec((1, DA_VDIM), lambda b, h, i: (0, h)),
        ],
        out_specs=pl.BlockSpec((tq, DA_VDIM), lambda b, h, i: (b * nq + i, h)),
        out_shape=jax.ShapeDtypeStruct((n, DA_HEADS * DA_VDIM), BF16),
        scratch_shapes=[
            pltpu.VMEM((DA_VDIM, seq), BF16),
            pltpu.VMEM((2, tq, 2 * tq), F32),
            pltpu.VMEM((DA_VDIM, 2 * tq), F32),
            pltpu.VMEM((1, 2 * tq), F32),
            pltpu.VMEM((1, 2 * tq), F32),
        ],
        compiler_params=_params(("parallel", "parallel", "arbitrary")),
        name="diffattn",
    )(slopes, z, z, z, lq1, lk1, lq2, lk2, gda)


def _outproj_kernel(x_ref, hml_ref, hda_ref, wo1_ref, wo2_ref, g_ref, wq_ref, x1_ref, q_ref):
    x1 = x_ref[...] + _mm(hml_ref[...], wo1_ref[...]) + _mm(hda_ref[...], wo2_ref[...])
    x1_ref[...] = x1
    q_ref[...] = _mm(_rms(x1, g_ref[...]).astype(BF16), wq_ref[...]).astype(BF16)


def _outproj(x2d, hml, hda, wo1, wo2, g, wq, tm):
    n, d = x2d.shape
    half = hml.shape[1]
    row = lambda i: (i, 0)
    fixed = lambda i: (0, 0)
    return pl.pallas_call(
        _outproj_kernel,
        grid=(n // tm,),
        in_specs=[
            pl.BlockSpec((tm, d), row),
            pl.BlockSpec((tm, half), row),
            pl.BlockSpec((tm, half), row),
            pl.BlockSpec((half, d), fixed),
            pl.BlockSpec((half, d), fixed),
            pl.BlockSpec((1, d), fixed),
            pl.BlockSpec((d, d), fixed),
        ],
        out_specs=[pl.BlockSpec((tm, d), row), pl.BlockSpec((tm, d), row)],
        out_shape=[jax.ShapeDtypeStruct((n, d), F32), jax.ShapeDtypeStruct((n, d), BF16)],
        compiler_params=_params(("parallel",)),
        name="outproj",
    )(x2d, hml, hda, wo1, wo2, g, wq)


def _memkv_kernel(m_ref, g_ref, wk_ref, wv_ref, k_ref, v_ref):
    mb = _rms(m_ref[...], g_ref[...]).astype(BF16)
    k_ref[...] = _mm(mb, wk_ref[...]).astype(BF16)
    v_ref[...] = _mm(mb, wv_ref[...]).astype(BF16)


def _memkv(mem2d, g, wk, wv, tm):
    n, d = mem2d.shape
    row = lambda i: (i, 0)
    fixed = lambda i: (0, 0)
    return pl.pallas_call(
        _memkv_kernel,
        grid=(n // tm,),
        in_specs=[pl.BlockSpec((tm, d), row), pl.BlockSpec((1, d), fixed),
                  pl.BlockSpec((d, d), fixed), pl.BlockSpec((d, d), fixed)],
        out_specs=[pl.BlockSpec((tm, d), row), pl.BlockSpec((tm, d), row)],
        out_shape=[jax.ShapeDtypeStruct((n, d), BF16), jax.ShapeDtypeStruct((n, d), BF16)],
        compiler_params=_params(("parallel",)),
        name="memkv",
    )(mem2d, g, wk, wv)


def _cross_kernel(q_ref, k_ref, v_ref, x1_ref, wco_ref, g_ref, wpq_ref, sk_ref,
                  x2_ref, eidx_ref, gate_ref):
    d = q_ref.shape[1]
    hd = d // CA_HEADS
    scale = hd ** -0.5
    outs = []
    for hh in range(CA_HEADS):
        cs = slice(hh * hd, (hh + 1) * hd)
        s = _mm_nt(q_ref[:, cs], k_ref[:, cs]) * scale
        s = s - jnp.max(s, axis=-1, keepdims=True)
        e = jnp.exp(s)
        p = e / jnp.sum(e, axis=-1, keepdims=True)
        outs.append(_mm(p.astype(BF16), v_ref[:, cs]).astype(BF16))
    o = jnp.concatenate(outs, axis=1)
    x2 = x1_ref[...] + _mm(o, wco_ref[...])
    x2_ref[...] = x2
    xn = _rms(x2, g_ref[...])
    qp = _mm(xn.astype(BF16), wpq_ref[...]).astype(BF16)
    lane = lax.broadcasted_iota(jnp.int32, (qp.shape[0], LANES), 1)
    gates = []
    eids = []
    for hp in range(PEER_HEADS):
        qh = qp[:, hp * LANES:(hp + 1) * LANES]
        zero = jnp.zeros_like(qh)
        sk = sk_ref[hp]
        g, e = _route_head(_mm_nt(sk, jnp.where(lane < PEER_HALF, qh, zero)),
                           _mm_nt(sk, jnp.where(lane >= PEER_HALF, qh, zero)))
        gates.append(g)
        eids.append(e)
    gate_ref[...] = jnp.concatenate(gates, axis=0).T
    eidx_ref[...] = jnp.concatenate(eids, axis=0).T


def _cross(qca, kca, vca, x1, wco, g, wpq, skcat, tm, seq, mem_len):
    n, d = x1.shape
    per_b = seq // tm
    row = lambda i: (i, 0)
    fixed = lambda i: (0, 0)
    memb = lambda i: (i // per_b, 0)
    width = PEER_HEADS * PEER_TOPK
    return pl.pallas_call(
        _cross_kernel,
        grid=(n // tm,),
        in_specs=[
            pl.BlockSpec((tm, d), row),
            pl.BlockSpec((mem_len, d), memb),
            pl.BlockSpec((mem_len, d), memb),
            pl.BlockSpec((tm, d), row),
            pl.BlockSpec((d, d), fixed),
            pl.BlockSpec((1, d), fixed),
            pl.BlockSpec((d, PEER_HEADS * LANES), fixed),
            pl.BlockSpec((PEER_HEADS, PEER_KEYS, LANES), lambda i: (0, 0, 0)),
        ],
        out_specs=[
            pl.BlockSpec((tm, d), row),
            pl.BlockSpec((tm, width), row),
            pl.BlockSpec((tm, width), row),
        ],
        out_shape=[
            jax.ShapeDtypeStruct((n, d), F32),
            jax.ShapeDtypeStruct((n, width), jnp.int32),
            jax.ShapeDtypeStruct((n, width), F32),
        ],
        compiler_params=_params(("parallel",)),
        name="cross",
    )(qca, kca, vca, x1, wco, g, wpq, skcat)


def _extract_topk(x, payload):
    rows, t = x.shape
    rid = lax.broadcasted_iota(jnp.int32, (rows, t), 0)
    krow = lax.broadcasted_iota(jnp.int32, (PEER_TOPK, t), 0)
    vals = jnp.zeros((PEER_TOPK, t), F32)
    picked = jnp.zeros((PEER_TOPK, t), jnp.int32)
    for r in range(PEER_TOPK):
        m = jnp.max(x, axis=0, keepdims=True)
        idx = jnp.min(jnp.where(x == m, rid, rows), axis=0, keepdims=True)
        hit = rid == idx
        if payload is None:
            sel = idx
        else:
            sel = jnp.sum(jnp.where(hit, payload, 0), axis=0, keepdims=True)
        vals = jnp.where(krow == r, m, vals)
        picked = jnp.where(krow == r, sel, picked)
        x = jnp.where(hit, -jnp.inf, x)
    return vals, picked


def _pair_candidates(va, ia, vb, ib):
    t = va.shape[1]
    sub = 8
    row = lax.broadcasted_iota(jnp.int32, (sub, t), 0)
    vals = [va[0:1] + vb]
    ids = [ia[0:1] * PEER_KEYS + ib]
    for i in range(1, sub):
        keep = PEER_TOPK // (i + 1)
        v = va[i:i + 1] + vb[:sub]
        vals.append(v if keep >= sub else jnp.where(row < keep, v, -jnp.inf))
        ids.append(ia[i:i + 1] * PEER_KEYS + ib[:sub])
    vals.append(va[sub:] + vb[0:1])
    ids.append(ia[sub:] * PEER_KEYS + ib[0:1])
    return jnp.concatenate(vals, axis=0), jnp.concatenate(ids, axis=0)


def _route_head(s0, s1):
    va, ia = _extract_topk(s0, None)
    vb, ib = _extract_topk(s1, None)
    cand, cid = _pair_candidates(va, ia, vb, ib)
    bv, be = _extract_topk(cand, cid)
    e = jnp.exp(bv - bv[0:1, :])
    return e / jnp.sum(e, axis=0, keepdims=True), be


PEER_SLOTS = 8
PEER_ROW_TILES = 16
PEER_PITCH = PEER_ROW_TILES + 1


def _peer_kernel(eidx_ref, gate_ref, x2_ref, gffn_ref, gfin_ref, uv_hbm, out_ref, buf, sem, acc_ref, xn_ref,
                 *, tb):
    nsel = PEER_HEADS * PEER_TOPK
    d = x2_ref.shape[1]
    xn_ref[...] = _rms(x2_ref[...], gffn_ref[...])
    i = pl.program_id(0)
    last_step = pl.num_programs(0) - 1
    ngroups = tb // PEER_SLOTS

    sub = PEER_ROW_TILES

    def issue(idx_ref, t, slot):
        for k in range(nsel):
            e8 = pl.multiple_of(idx_ref[t, k] * sub, sub)
            dst = buf.at[pl.ds((slot * nsel + k) * PEER_PITCH, sub), :]
            pltpu.make_async_copy(uv_hbm.at[pl.ds(e8, sub), :], dst, sem.at[slot]).start(priority=k % 2)

    def wait_slot(slot):
        whole = buf.at[pl.ds(0, nsel * sub), :]
        pltpu.make_async_copy(whole, whole, sem.at[slot]).wait()

    eye = (lax.broadcasted_iota(jnp.int32, (nsel, nsel), 0) == lax.broadcasted_iota(jnp.int32, (nsel, nsel), 1))

    tiles = d // LANES
    per_vreg = 8

    def load_rows(s, half):
        groups = []
        for g in range(nsel // per_vreg):
            base = (s * nsel + g * per_vreg) * PEER_PITCH + half * tiles
            groups.append(jnp.concatenate(
                [buf[pl.ds(base + c, per_vreg, stride=PEER_PITCH), :] for c in range(tiles)], axis=1))
        return jnp.concatenate(groups, axis=0)

    def compute(t, s):
        x = xn_ref[pl.ds(t, 1), :]
        u = load_rows(s, 0)
        act = jnp.sum(u * x, axis=-1, keepdims=True)
        g_row = gate_ref[pl.ds(t, 1), :]
        g_col = jnp.sum(jnp.where(eye, g_row, 0.0), axis=-1, keepdims=True)
        coef = 0.5 * act * (1.0 + lax.erf(act * (2.0 ** -0.5))) * g_col
        v = load_rows(s, 1)
        return jnp.sum(coef * v, axis=0, keepdims=True)

    @pl.when(i == 0)
    def _():
        for s in range(PEER_SLOTS - 1):
            issue(eidx_ref, s, s)

    def group(g, carry):
        for s in range(PEER_SLOTS):
            t = g * PEER_SLOTS + s
            wait_slot(s)
            row = compute(t, s)
            issue(eidx_ref, t + PEER_SLOTS - 1, (s - 1) % PEER_SLOTS)
            acc_ref[pl.ds(t, 1), :] = row
        return carry

    lax.fori_loop(0, ngroups, group, 0)

    @pl.when(i == last_step)
    def _():
        for s in range(PEER_SLOTS - 1):
            wait_slot(s)

    out_ref[...] = _rms(x2_ref[...] + acc_ref[...], gfin_ref[...])


def _pack_experts(peer_u, peer_v):
    e, d = peer_u.shape
    both = jnp.concatenate([peer_u.reshape(e, d // LANES, LANES), peer_v.reshape(e, d // LANES, LANES)], axis=1)
    return both.reshape(e * PEER_ROW_TILES, LANES)


def _peer(eidx, gate, x2, gffn, gfin, uv, tb):
    n, d = x2.shape
    nsel = PEER_HEADS * PEER_TOPK
    nsteps = n // tb
    row = lambda i: (i, 0)
    blocks = eidx.reshape(nsteps, tb, nsel)
    heads = jnp.concatenate([blocks[1:, :PEER_SLOTS], jnp.zeros((1, PEER_SLOTS, nsel), eidx.dtype)], axis=0)
    eidx_ext = jnp.concatenate([blocks, heads], axis=1)
    return pl.pallas_call(
        functools.partial(_peer_kernel, tb=tb),
        grid=(nsteps,),
        in_specs=[
            pl.BlockSpec((None, tb + PEER_SLOTS, nsel), lambda i: (i, 0, 0), memory_space=pltpu.SMEM),
            pl.BlockSpec((tb, nsel), row),
            pl.BlockSpec((tb, d), row),
            pl.BlockSpec((1, d), lambda i: (0, 0)),
            pl.BlockSpec((1, d), lambda i: (0, 0)),
            pl.BlockSpec(memory_space=pl.ANY),
        ],
        out_specs=pl.BlockSpec((tb, d), row),
        out_shape=jax.ShapeDtypeStruct((n, d), F32),
        scratch_shapes=[
            pltpu.VMEM((PEER_SLOTS * nsel * PEER_PITCH, LANES), F32),
            pltpu.SemaphoreType.DMA((PEER_SLOTS,)),
            pltpu.VMEM((tb, d), F32),
            pltpu.VMEM((tb, d), F32),
        ],
        compiler_params=_params(("arbitrary",)),
        name="peer",
    )(eidx_ext, gate, x2, gffn, gfin, uv)


def _layer(x2d, mem2d, batch, seq, mem_len, layer_idx, norm_mix_g, w_in, conv_w, b_igate, b_fgate,
           ml_norm_g, lambda_q1, lambda_k1, lambda_q2, lambda_k2, da_norm_g, w_out, norm_ca_g,
           norm_mem_g, w_cq, w_ck, w_cv, w_co, norm_ffn_g, w_pq, sub_keys, peer_u, peer_v, final_g):
    n, d = x2d.shape
    mlw = ML_HEADS * ML_HEAD_DIM
    gate0 = 4 * mlw
    gate1 = gate0 + 2 * ML_HEADS
    tm = min(512, n)

    w_main = jnp.concatenate([w_in[:, :gate0], w_in[:, gate1:]], axis=1).astype(BF16)
    w_gate = jnp.pad(w_in[:, gate0:gate1], ((0, 0), (0, LANES - 2 * ML_HEADS))).astype(BF16)
    gbias = jnp.pad(jnp.concatenate([b_igate, b_fgate]), (0, LANES - 2 * ML_HEADS)).reshape(1, LANES)
    row = lambda a: a.reshape(1, -1)

    z, gates = _inproj(x2d, row(norm_mix_g), w_main, w_gate, tm)
    hml = _mlstm(z, gates, conv_w, gbias, row(ml_norm_g), batch, seq)

    lam_init = 0.8 - 0.6 * math.exp(-0.3 * layer_idx)
    slopes = jnp.asarray(np.array([2.0 ** (-8.0 * (i + 1) / DA_HEADS) for i in range(DA_HEADS)], np.float32))
    hda = _diffattn(z, slopes, row(lambda_q1), row(lambda_k1), row(lambda_q2), row(lambda_k2),
                    row(da_norm_g), batch, seq, min(512, seq), lam_init)

    wo = w_out.astype(BF16)
    x1, qca = _outproj(x2d, hml, hda, wo[:mlw], wo[mlw:], row(norm_ca_g), w_cq.astype(BF16), tm)
    kca, vca = _memkv(mem2d, row(norm_mem_g), w_ck.astype(BF16), w_cv.astype(BF16), min(512, mem2d.shape[0]))

    skcat = sub_keys.transpose(0, 2, 1, 3).reshape(PEER_HEADS, PEER_KEYS, 2 * PEER_HALF).astype(BF16)
    x2, eidx, gate = _cross(qca, kca, vca, x1, w_co.astype(BF16), row(norm_ffn_g), w_pq.astype(BF16), skcat,
                        min(256, seq), seq, mem_len)
    return _peer(eidx, gate, x2, row(norm_ffn_g), row(final_g), _pack_experts(peer_u, peer_v), min(128, n))


def kernel(x, mem, norm_mix_g, w_in, conv_w, b_igate, b_fgate, ml_norm_g, lambda_q1, lambda_k1, lambda_q2, lambda_k2, da_norm_g, w_out, norm_ca_g, norm_mem_g, w_cq, w_ck, w_cv, w_co, norm_ffn_g, w_pq, sub_keys, peer_u, peer_v, final_norm_g):
    batch, seq, d = x.shape
    mem_len = mem.shape[1]
    depth = w_in.shape[0]
    assert depth == 1, "final norm is fused into the last layer's PEER kernel; one layer supported"
    x2d = x.reshape(batch * seq, d)
    mem2d = mem.reshape(batch * mem_len, d)
    out = _layer(x2d, mem2d, batch, seq, mem_len, 0, norm_mix_g[0], w_in[0], conv_w[0], b_igate[0],
                 b_fgate[0], ml_norm_g[0], lambda_q1[0], lambda_k1[0], lambda_q2[0], lambda_k2[0],
                 da_norm_g[0], w_out[0], norm_ca_g[0], norm_mem_g[0], w_cq[0], w_ck[0], w_cv[0],
                 w_co[0], norm_ffn_g[0], w_pq[0], sub_keys[0], peer_u[0], peer_v[0], final_norm_g)
    return out.reshape(batch, seq, d)
```

```python
import functools
import math

import jax
import jax.numpy as jnp
import numpy as np
from jax import lax
from jax.experimental import pallas as pl
from jax.experimental.pallas import tpu as pltpu

F32 = jnp.float32
BF16 = jnp.bfloat16
EPS = 1e-6

ML_HEADS = 4
ML_HEAD_DIM = 128
ML_CHUNK = 64
CONV_WIDTH = 4
DA_HEADS = 4
DA_QK_DIM = 64
DA_VDIM = 128
CA_HEADS = 4
PEER_HEADS = 8
PEER_KEYS = 128
PEER_TOPK = 16
PEER_HALF = 64
LANES = 128

VMEM_LIMIT = 56 * 1024 * 1024


def _mm(a, b):
    return jnp.dot(a, b, preferred_element_type=F32)


def _mm_nt(a, b):
    return lax.dot_general(a, b, (((1,), (1,)), ((), ())), preferred_element_type=F32)


def _rms(x, g):
    return x * lax.rsqrt(jnp.mean(x * x, axis=-1, keepdims=True) + EPS) * g


def _params(sem):
    return pltpu.CompilerParams(dimension_semantics=sem, vmem_limit_bytes=VMEM_LIMIT)


def _inproj_kernel(x_ref, g_ref, w_ref, wg_ref, z_ref, gate_ref):
    hb = _rms(x_ref[...], g_ref[...]).astype(BF16)
    z_ref[...] = _mm(hb, w_ref[...]).astype(BF16)
    gate_ref[...] = _mm(hb, wg_ref[...])


def _inproj(x2d, g, w_main, w_gate, tm):
    n, d = x2d.shape
    wz = w_main.shape[1]
    return pl.pallas_call(
        _inproj_kernel,
        grid=(n // tm,),
        in_specs=[
            pl.BlockSpec((tm, d), lambda i: (i, 0)),
            pl.BlockSpec((1, d), lambda i: (0, 0)),
            pl.BlockSpec((d, wz), lambda i: (0, 0)),
            pl.BlockSpec((d, LANES), lambda i: (0, 0)),
        ],
        out_specs=[
            pl.BlockSpec((tm, wz), lambda i: (i, 0)),
            pl.BlockSpec((tm, LANES), lambda i: (i, 0)),
        ],
        out_shape=[
            jax.ShapeDtypeStruct((n, wz), BF16),
            jax.ShapeDtypeStruct((n, LANES), F32),
        ],
        compiler_params=_params(("parallel",)),
        name="inproj",
    )(x2d, g, w_main, w_gate)


def _mlstm_kernel(zqk_ref, zv_ref, zo_ref, gates_ref, convw_ref, gbias_ref, gml_ref, out_ref,
                  qk_scr, c_scr, *, seq, nb):
    width = ML_HEADS * ML_HEAD_DIM
    conv_rows = 128
    halo = 16

    w = convw_ref[...]
    col = lax.broadcasted_iota(jnp.int32, (1, 2 * width), 1)
    kscale = jnp.where(col >= width, ML_HEAD_DIM ** -0.5, 1.0).astype(F32)

    def conv_body(rb, _):
        r0 = pl.multiple_of(rb * conv_rows, conv_rows)
        main = zqk_ref[pl.ds(r0, conv_rows), :].astype(F32)
        prev0 = pl.multiple_of(jnp.maximum(r0 - halo, 0), halo)
        prev = zqk_ref[pl.ds(prev0, halo), :].astype(F32)
        prev = jnp.where(rb % (seq // conv_rows) > 0, prev, 0.0)
        xx = jnp.concatenate([prev, main], axis=0)
        acc = xx[halo:] * w[CONV_WIDTH - 1:CONV_WIDTH, :]
        for j in range(CONV_WIDTH - 1):
            sh = CONV_WIDTH - 1 - j
            acc = acc + pltpu.roll(xx, sh, 0)[halo:] * w[j:j + 1, :]
        y = acc * jax.nn.sigmoid(acc) * kscale
        qk_scr[pl.ds(r0, conv_rows), :] = y.astype(BF16)
        return 0

    lax.fori_loop(0, nb * seq // conv_rows, conv_body, 0)

    c_scr[...] = jnp.zeros_like(c_scr)
    L = ML_CHUNK
    row = lax.broadcasted_iota(jnp.int32, (L, L), 0)
    colL = lax.broadcasted_iota(jnp.int32, (L, L), 1)
    tril = (colL <= row)
    tril_f = tril.astype(F32)
    lane128 = lax.broadcasted_iota(jnp.int32, (L, LANES), 1)
    ones_col = jnp.where(lane128 == 0, 1.0, 0.0).astype(BF16)
    gbias = gbias_ref[...]
    gml = gml_ref[...]

    def chunk_body(c, m_states):
        new_states = []
        for bb in range(nb):
            r0 = pl.multiple_of(bb * seq + c * L, L)
            gb = gates_ref[pl.ds(r0, L), :] + gbias
            lf = jnp.minimum(gb, 0.0) - jnp.log1p(jnp.exp(-jnp.abs(gb)))
            p = jnp.where(lane128 < ML_HEADS, gb, lf)
            cum = jnp.dot(tril_f, p, preferred_element_type=F32, precision=lax.Precision.HIGHEST)
            q_mix = jnp.where(lane128 < ML_HEADS, p, cum)
            qt = q_mix.T
            for h in range(ML_HEADS):
                m_st = m_states[bb * ML_HEADS + h]
                ic_col = p[:, h:h + 1]
                b_col = cum[:, ML_HEADS + h:ML_HEADS + h + 1]
                ic_row = qt[h:h + 1, :]
                b_row = qt[ML_HEADS + h:ML_HEADS + h + 1, :]
                b_last = b_col[L - 1:L, :]
                log_d = jnp.where(tril, b_col - b_row + ic_row, -jnp.inf)
                m_inter = b_col + m_st
                m_t = jnp.maximum(m_inter, jnp.max(log_d, axis=-1, keepdims=True))
                dmat = jnp.exp(log_d - m_t)
                hs = slice(h * ML_HEAD_DIM, (h + 1) * ML_HEAD_DIM)
                ks = slice(width + h * ML_HEAD_DIM, width + (h + 1) * ML_HEAD_DIM)
                qc = qk_scr[pl.ds(r0, L), hs]
                kc = qk_scr[pl.ds(r0, L), ks]
                vc = zv_ref[pl.ds(r0, L), hs]
                v_aug = jnp.concatenate([vc, ones_col], axis=1)
                sc = _mm_nt(qc, kc) * dmat
                inter = jnp.exp(m_inter - m_t)
                c_old = c_scr[bb * ML_HEADS + h]
                num_aug = _mm(sc.astype(BF16), v_aug) + inter * _mm(qc, c_old.astype(BF16))
                num = num_aug[:, :ML_HEAD_DIM]
                den = num_aug[:, ML_HEAD_DIM:ML_HEAD_DIM + 1]
                hraw = num / jnp.maximum(jnp.abs(den), jnp.exp(-m_t))
                hn = _rms(hraw, gml[:, hs])
                og = zo_ref[pl.ds(r0, L), hs].astype(F32)
                out_ref[pl.ds(r0, L), hs] = (hn * jax.nn.sigmoid(og)).astype(BF16)
                g_col = b_last - b_col + ic_col
                m_next = jnp.maximum(b_last + m_st, jnp.max(g_col, axis=0, keepdims=True))
                decay = jnp.exp(b_last + m_st - m_next)
                wgt = jnp.exp(g_col - m_next)
                kw_t = (wgt * kc.astype(F32)).T.astype(BF16)
                c_scr[bb * ML_HEADS + h] = decay * c_old + _mm(kw_t, v_aug)
                new_states.append(m_next)
        return tuple(new_states)

    init = tuple(jnp.zeros((1, 1), F32) for _ in range(nb * ML_HEADS))
    lax.fori_loop(0, seq // L, chunk_body, init)


def _mlstm(z, gates, conv_w, gbias, gml, batch, seq):
    n = batch * seq
    width = ML_HEADS * ML_HEAD_DIM
    nb = 2 if batch % 2 == 0 else 1
    rows = nb * seq
    return pl.pallas_call(
        functools.partial(_mlstm_kernel, seq=seq, nb=nb),
        grid=(batch // nb,),
        in_specs=[
            pl.BlockSpec((rows, 2 * width), lambda b: (b, 0)),
            pl.BlockSpec((rows, width), lambda b: (b, 2)),
            pl.BlockSpec((rows, width), lambda b: (b, 3)),
            pl.BlockSpec((rows, LANES), lambda b: (b, 0)),
            pl.BlockSpec((CONV_WIDTH, 2 * width), lambda b: (0, 0)),
            pl.BlockSpec((1, LANES), lambda b: (0, 0)),
            pl.BlockSpec((1, width), lambda b: (0, 0)),
        ],
        out_specs=pl.BlockSpec((rows, width), lambda b: (b, 0)),
        out_shape=jax.ShapeDtypeStruct((n, width), BF16),
        scratch_shapes=[
            pltpu.VMEM((rows, 2 * width), BF16),
            pltpu.VMEM((nb * ML_HEADS, ML_HEAD_DIM, 2 * ML_HEAD_DIM), F32),
        ],
        compiler_params=_params(("parallel",)),
        name="mlstm",
    )(z, z, z, gates, conv_w, gbias, gml)


def _diffattn_kernel(slopes_ref, q_ref, k_ref, v_ref, lq1_ref, lk1_ref, lq2_ref, lk2_ref, g_ref,
                     o_ref, vt_ref, bias_ref, acc_ref, m_ref, l_ref, *, tq, lam_init):
    tk = tq
    h = pl.program_id(1)
    i = pl.program_id(2)
    slope = slopes_ref[h]
    scale = DA_QK_DIM ** -0.5

    @pl.when(i == 0)
    def _():
        vt_ref[...] = v_ref[...].astype(F32).T.astype(BF16)
        kk = lax.broadcasted_iota(jnp.int32, (tk, 2 * tq), 0)
        qq = lax.broadcasted_iota(jnp.int32, (tk, 2 * tq), 1) & (tq - 1)
        kmq = kk - qq
        rel = kmq.astype(F32) * (-slope)
        bias_ref[0] = rel
        bias_ref[1] = jnp.where(kmq <= 0, rel, jnp.inf)

    q = (q_ref[...].astype(F32) * scale).astype(BF16)
    lane = lax.broadcasted_iota(jnp.int32, q.shape, 1)
    zero = jnp.zeros_like(q)
    q2 = jnp.concatenate([jnp.where(lane < DA_QK_DIM, q, zero), jnp.where(lane >= DA_QK_DIM, q, zero)], axis=0)
    m_ref[...] = jnp.full(m_ref.shape, -jnp.inf, F32)
    l_ref[...] = jnp.zeros(l_ref.shape, F32)
    acc_ref[...] = jnp.zeros(acc_ref.shape, F32)

    def block(j, masked):
        k0 = pl.multiple_of(j * tk, tk)
        ks = k_ref[pl.ds(k0, tk), :]
        off = ((i - j) * tq).astype(F32) * slope
        s = _mm_nt(ks, q2) - bias_ref[1 if masked else 0]
        m_old = m_ref[...]
        m_new = jnp.maximum(m_old, jnp.max(s, axis=0, keepdims=True) - off)
        alpha = jnp.exp(m_old - m_new)
        p = jnp.exp(s - (m_new + off))
        l_ref[...] = alpha * l_ref[...] + jnp.sum(p, axis=0, keepdims=True)
        acc_ref[...] = alpha * acc_ref[...] + _mm(vt_ref[:, pl.ds(k0, tk)], p.astype(BF16))
        m_ref[...] = m_new

    def full_body(j, carry):
        block(j, False)
        return carry

    lax.fori_loop(0, i, full_body, 0)
    block(i, True)

    lam = (jnp.exp(jnp.sum(lq1_ref[...] * lk1_ref[...], axis=-1, keepdims=True))
           - jnp.exp(jnp.sum(lq2_ref[...] * lk2_ref[...], axis=-1, keepdims=True)) + lam_init)
    ot = acc_ref[:, :tq] / l_ref[:, :tq] - lam * (acc_ref[:, tq:] / l_ref[:, tq:])
    o_ref[...] = (_rms(ot.T, g_ref[...]) * (1.0 - lam_init)).astype(BF16)


def _diffattn(z, slopes, lq1, lk1, lq2, lk2, gda, batch, seq, tq, lam_init):
    n = batch * seq
    nq = seq // tq
    qcol0 = (4 * ML_HEADS * ML_HEAD_DIM) // LANES
    kcol0 = qcol0 + (DA_HEADS * DA_VDIM) // LANES
    vcol0 = kcol0 + (DA_HEADS * DA_VDIM) // LANES
    lam_spec = pl.BlockSpec((1, DA_QK_DIM), lambda b, h, i: (0, 0))
    return pl.pallas_call(
        functools.partial(_diffattn_kernel, tq=tq, lam_init=lam_init),
        grid=(batch, DA_HEADS, nq),
        in_specs=[
            pl.BlockSpec(memory_space=pltpu.SMEM),
            pl.BlockSpec((tq, LANES), lambda b, h, i: (b * nq + i, qcol0 + h)),
            pl.BlockSpec((seq, LANES), lambda b, h, i: (b, kcol0 + h)),
            pl.BlockSpec((seq, LANES), lambda b, h, i: (b, vcol0 + h)),
            lam_spec, lam_spec, lam_spec, lam_spec,
            pl.BlockSpec((1, DA_VDIM), lambda b, h, i: (0, h)),
        ],
        out_specs=pl.BlockSpec((tq, DA_VDIM), lambda b, h, i: (b * nq + i, h)),
        out_shape=jax.ShapeDtypeStruct((n, DA_HEADS * DA_VDIM), BF16),
        scratch_shapes=[
            pltpu.VMEM((DA_VDIM, seq), BF16),
            pltpu.VMEM((2, tq, 2 * tq), F32),
            pltpu.VMEM((DA_VDIM, 2 * tq), F32),
            pltpu.VMEM((1, 2 * tq), F32),
            pltpu.VMEM((1, 2 * tq), F32),
        ],
        compiler_params=_params(("parallel", "parallel", "arbitrary")),
        name="diffattn",
    )(slopes, z, z, z, lq1, lk1, lq2, lk2, gda)


def _outproj_kernel(x_ref, hml_ref, hda_ref, wo1_ref, wo2_ref, g_ref, wq_ref, x1_ref, q_ref):
    x1 = x_ref[...] + _mm(hml_ref[...], wo1_ref[...]) + _mm(hda_ref[...], wo2_ref[...])
    x1_ref[...] = x1
    q_ref[...] = _mm(_rms(x1, g_ref[...]).astype(BF16), wq_ref[...]).astype(BF16)


def _outproj(x2d, hml, hda, wo1, wo2, g, wq, tm):
    n, d = x2d.shape
    half = hml.shape[1]
    row = lambda i: (i, 0)
    fixed = lambda i: (0, 0)
    return pl.pallas_call(
        _outproj_kernel,
        grid=(n // tm,),
        in_specs=[
            pl.BlockSpec((tm, d), row),
            pl.BlockSpec((tm, half), row),
            pl.BlockSpec((tm, half), row),
            pl.BlockSpec((half, d), fixed),
            pl.BlockSpec((half, d), fixed),
            pl.BlockSpec((1, d), fixed),
            pl.BlockSpec((d, d), fixed),
        ],
        out_specs=[pl.BlockSpec((tm, d), row), pl.BlockSpec((tm, d), row)],
        out_shape=[jax.ShapeDtypeStruct((n, d), F32), jax.ShapeDtypeStruct((n, d), BF16)],
        compiler_params=_params(("parallel",)),
        name="outproj",
    )(x2d, hml, hda, wo1, wo2, g, wq)


def _memkv_kernel(m_ref, g_ref, wk_ref, wv_ref, k_ref, v_ref):
    mb = _rms(m_ref[...], g_ref[...]).astype(BF16)
    k_ref[...] = _mm(mb, wk_ref[...]).astype(BF16)
    v_ref[...] = _mm(mb, wv_ref[...]).astype(BF16)


def _memkv(mem2d, g, wk, wv, tm):
    n, d = mem2d.shape
    row = lambda i: (i, 0)
    fixed = lambda i: (0, 0)
    return pl.pallas_call(
        _memkv_kernel,
        grid=(n // tm,),
        in_specs=[pl.BlockSpec((tm, d), row), pl.BlockSpec((1, d), fixed),
                  pl.BlockSpec((d, d), fixed), pl.BlockSpec((d, d), fixed)],
        out_specs=[pl.BlockSpec((tm, d), row), pl.BlockSpec((tm, d), row)],
        out_shape=[jax.ShapeDtypeStruct((n, d), BF16), jax.ShapeDtypeStruct((n, d), BF16)],
        compiler_params=_params(("parallel",)),
        name="memkv",
    )(mem2d, g, wk, wv)


def _cross_kernel(q_ref, k_ref, v_ref, x1_ref, wco_ref, g_ref, wpq_ref, sk_ref,
                  x2_ref, eidx_ref, gate_ref):
    d = q_ref.shape[1]
    hd = d // CA_HEADS
    scale = hd ** -0.5
    outs = []
    for hh in range(CA_HEADS):
        cs = slice(hh * hd, (hh + 1) * hd)
        s = _mm_nt(q_ref[:, cs], k_ref[:, cs]) * scale
        s = s - jnp.max(s, axis=-1, keepdims=True)
        e = jnp.exp(s)
        p = e / jnp.sum(e, axis=-1, keepdims=True)
        outs.append(_mm(p.astype(BF16), v_ref[:, cs]).astype(BF16))
    o = jnp.concatenate(outs, axis=1)
    x2 = x1_ref[...] + _mm(o, wco_ref[...])
    x2_ref[...] = x2
    xn = _rms(x2, g_ref[...])
    qp = _mm(xn.astype(BF16), wpq_ref[...]).astype(BF16)
    lane = lax.broadcasted_iota(jnp.int32, (qp.shape[0], LANES), 1)
    gates = []
    eids = []
    for hp in range(PEER_HEADS):
        qh = qp[:, hp * LANES:(hp + 1) * LANES]
        zero = jnp.zeros_like(qh)
        sk = sk_ref[hp]
        g, e = _route_head(_mm_nt(sk, jnp.where(lane < PEER_HALF, qh, zero)),
                           _mm_nt(sk, jnp.where(lane >= PEER_HALF, qh, zero)))
        gates.append(g)
        eids.append(e)
    gate_ref[...] = jnp.concatenate(gates, axis=0).T
    eidx_ref[...] = jnp.concatenate(eids, axis=0).T


def _cross(qca, kca, vca, x1, wco, g, wpq, skcat, tm, seq, mem_len):
    n, d = x1.shape
    per_b = seq // tm
    row = lambda i: (i, 0)
    fixed = lambda i: (0, 0)
    memb = lambda i: (i // per_b, 0)
    width = PEER_HEADS * PEER_TOPK
    return pl.pallas_call(
        _cross_kernel,
        grid=(n // tm,),
        in_specs=[
            pl.BlockSpec((tm, d), row),
            pl.BlockSpec((mem_len, d), memb),
            pl.BlockSpec((mem_len, d), memb),
            pl.BlockSpec((tm, d), row),
            pl.BlockSpec((d, d), fixed),
            pl.BlockSpec((1, d), fixed),
            pl.BlockSpec((d, PEER_HEADS * LANES), fixed),
            pl.BlockSpec((PEER_HEADS, PEER_KEYS, LANES), lambda i: (0, 0, 0)),
        ],
        out_specs=[
            pl.BlockSpec((tm, d), row),
            pl.BlockSpec((tm, width), row),
            pl.BlockSpec((tm, width), row),
        ],
        out_shape=[
            jax.ShapeDtypeStruct((n, d), F32),
            jax.ShapeDtypeStruct((n, width), jnp.int32),
            jax.ShapeDtypeStruct((n, width), F32),
        ],
        compiler_params=_params(("parallel",)),
        name="cross",
    )(qca, kca, vca, x1, wco, g, wpq, skcat)


def _extract_topk(x, payload):
    rows, t = x.shape
    rid = lax.broadcasted_iota(jnp.int32, (rows, t), 0)
    krow = lax.broadcasted_iota(jnp.int32, (PEER_TOPK, t), 0)
    vals = jnp.zeros((PEER_TOPK, t), F32)
    picked = jnp.zeros((PEER_TOPK, t), jnp.int32)
    for r in range(PEER_TOPK):
        m = jnp.max(x, axis=0, keepdims=True)
        idx = jnp.min(jnp.where(x == m, rid, rows), axis=0, keepdims=True)
        hit = rid == idx
        if payload is None:
            sel = idx
        else:
            sel = jnp.sum(jnp.where(hit, payload, 0), axis=0, keepdims=True)
        vals = jnp.where(krow == r, m, vals)
        picked = jnp.where(krow == r, sel, picked)
        x = jnp.where(hit, -jnp.inf, x)
    return vals, picked


def _pair_candidates(va, ia, vb, ib):
    t = va.shape[1]
    sub = 8
    row = lax.broadcasted_iota(jnp.int32, (sub, t), 0)
    vals = [va[0:1] + vb]
    ids = [ia[0:1] * PEER_KEYS + ib]
    for i in range(1, sub):
        keep = PEER_TOPK // (i + 1)
        v = va[i:i + 1] + vb[:sub]
        vals.append(v if keep >= sub else jnp.where(row < keep, v, -jnp.inf))
        ids.append(ia[i:i + 1] * PEER_KEYS + ib[:sub])
    vals.append(va[sub:] + vb[0:1])
    ids.append(ia[sub:] * PEER_KEYS + ib[0:1])
    return jnp.concatenate(vals, axis=0), jnp.concatenate(ids, axis=0)


def _route_head(s0, s1):
    va, ia = _extract_topk(s0, None)
    vb, ib = _extract_topk(s1, None)
    cand, cid = _pair_candidates(va, ia, vb, ib)
    bv, be = _extract_topk(cand, cid)
    e = jnp.exp(bv - bv[0:1, :])
    return e / jnp.sum(e, axis=0, keepdims=True), be


PEER_SLOTS = 8
PEER_ROW_TILES = 16
PEER_PITCH = PEER_ROW_TILES + 1


def _peer_kernel(eidx_ref, gate_ref, x2_ref, gffn_ref, gfin_ref, uv_hbm, out_ref, buf, sem, acc_ref, xn_ref,
                 *, tb):
    nsel = PEER_HEADS * PEER_TOPK
    d = x2_ref.shape[1]
    xn_ref[...] = _rms(x2_ref[...], gffn_ref[...])
    i = pl.program_id(0)
    last_step = pl.num_programs(0) - 1
    ngroups = tb // PEER_SLOTS

    sub = PEER_ROW_TILES

    def issue(idx_ref, t, slot):
        for k in range(nsel):
            e8 = pl.multiple_of(idx_ref[t, k] * sub, sub)
            dst = buf.at[pl.ds((slot * nsel + k) * PEER_PITCH, sub), :]
            pltpu.make_async_copy(uv_hbm.at[pl.ds(e8, sub), :], dst, sem.at[slot]).start(priority=k % 2)

    def wait_slot(slot):
        whole = buf.at[pl.ds(0, nsel * sub), :]
        pltpu.make_async_copy(whole, whole, sem.at[slot]).wait()

    eye = (lax.broadcasted_iota(jnp.int32, (nsel, nsel), 0) == lax.broadcasted_iota(jnp.int32, (nsel, nsel), 1))

    tiles = d // LANES
    per_vreg = 8

    def load_rows(s, half):
        groups = []
        for g in range(nsel // per_vreg):
            base = (s * nsel + g * per_vreg) * PEER_PITCH + half * tiles
            groups.append(jnp.concatenate(
                [buf[pl.ds(base + c, per_vreg, stride=PEER_PITCH), :] for c in range(tiles)], axis=1))
        return jnp.concatenate(groups, axis=0)

    def compute(t, s):
        x = xn_ref[pl.ds(t, 1), :]
        u = load_rows(s, 0)
        act = jnp.sum(u * x, axis=-1, keepdims=True)
        g_row = gate_ref[pl.ds(t, 1), :]
        g_col = jnp.sum(jnp.where(eye, g_row, 0.0), axis=-1, keepdims=True)
        coef = 0.5 * act * (1.0 + lax.erf(act * (2.0 ** -0.5))) * g_col
        v = load_rows(s, 1)
        return jnp.sum(coef * v, axis=0, keepdims=True)

    @pl.when(i == 0)
    def _():
        for s in range(PEER_SLOTS - 1):
            issue(eidx_ref, s, s)

    def group(g, carry):
        for s in range(PEER_SLOTS):
            t = g * PEER_SLOTS + s
            wait_slot(s)
            row = compute(t, s)
            issue(eidx_ref, t + PEER_SLOTS - 1, (s - 1) % PEER_SLOTS)
            acc_ref[pl.ds(t, 1), :] = row
        return carry

    lax.fori_loop(0, ngroups, group, 0)

    @pl.when(i == last_step)
    def _():
        for s in range(PEER_SLOTS - 1):
            wait_slot(s)

    out_ref[...] = _rms(x2_ref[...] + acc_ref[...], gfin_ref[...])


def _pack_experts(peer_u, peer_v):
    e, d = peer_u.shape
    both = jnp.concatenate([peer_u.reshape(e, d // LANES, LANES), peer_v.reshape(e, d // LANES, LANES)], axis=1)
    return both.reshape(e * PEER_ROW_TILES, LANES)


def _peer(eidx, gate, x2, gffn, gfin, uv, tb):
    n, d = x2.shape
    nsel = PEER_HEADS * PEER_TOPK
    nsteps = n // tb
    row = lambda i: (i, 0)
    blocks = eidx.reshape(nsteps, tb, nsel)
    heads = jnp.concatenate([blocks[1:, :PEER_SLOTS], jnp.zeros((1, PEER_SLOTS, nsel), eidx.dtype)], axis=0)
    eidx_ext = jnp.concatenate([blocks, heads], axis=1)
    return pl.pallas_call(
        functools.partial(_peer_kernel, tb=tb),
        grid=(nsteps,),
        in_specs=[
            pl.BlockSpec((None, tb + PEER_SLOTS, nsel), lambda i: (i, 0, 0), memory_space=pltpu.SMEM),
            pl.BlockSpec((tb, nsel), row),
            pl.BlockSpec((tb, d), row),
            pl.BlockSpec((1, d), lambda i: (0, 0)),
            pl.BlockSpec((1, d), lambda i: (0, 0)),
            pl.BlockSpec(memory_space=pl.ANY),
        ],
        out_specs=pl.BlockSpec((tb, d), row),
        out_shape=jax.ShapeDtypeStruct((n, d), F32),
        scratch_shapes=[
            pltpu.VMEM((PEER_SLOTS * nsel * PEER_PITCH, LANES), F32),
            pltpu.SemaphoreType.DMA((PEER_SLOTS,)),
            pltpu.VMEM((tb, d), F32),
            pltpu.VMEM((tb, d), F32),
        ],
        compiler_params=_params(("arbitrary",)),
        name="peer",
    )(eidx_ext, gate, x2, gffn, gfin, uv)


def _layer(x2d, mem2d, batch, seq, mem_len, layer_idx, norm_mix_g, w_in, conv_w, b_igate, b_fgate,
           ml_norm_g, lambda_q1, lambda_k1, lambda_q2, lambda_k2, da_norm_g, w_out, norm_ca_g,
           norm_mem_g, w_cq, w_ck, w_cv, w_co, norm_ffn_g, w_pq, sub_keys, peer_u, peer_v, final_g):
    n, d = x2d.shape
    mlw = ML_HEADS * ML_HEAD_DIM
    gate0 = 4 * mlw
    gate1 = gate0 + 2 * ML_HEADS
    tm = min(512, n)

    w_main = jnp.concatenate([w_in[:, :gate0], w_in[:, gate1:]], axis=1).astype(BF16)
    w_gate = jnp.pad(w_in[:, gate0:gate1], ((0, 0), (0, LANES - 2 * ML_HEADS))).astype(BF16)
    gbias = jnp.pad(jnp.concatenate([b_igate, b_fgate]), (0, LANES - 2 * ML_HEADS)).reshape(1, LANES)
    row = lambda a: a.reshape(1, -1)

    z, gates = _inproj(x2d, row(norm_mix_g), w_main, w_gate, tm)
    hml = _mlstm(z, gates, conv_w, gbias, row(ml_norm_g), batch, seq)

    lam_init = 0.8 - 0.6 * math.exp(-0.3 * layer_idx)
    slopes = jnp.asarray(np.array([2.0 ** (-8.0 * (i + 1) / DA_HEADS) for i in range(DA_HEADS)], np.float32))
    hda = _diffattn(z, slopes, row(lambda_q1), row(lambda_k1), row(lambda_q2), row(lambda_k2),
                    row(da_norm_g), batch, seq, min(512, seq), lam_init)

    wo = w_out.astype(BF16)
    x1, qca = _outproj(x2d, hml, hda, wo[:mlw], wo[mlw:], row(norm_ca_g), w_cq.astype(BF16), tm)
    kca, vca = _memkv(mem2d, row(norm_mem_g), w_ck.astype(BF16), w_cv.astype(BF16), min(512, mem2d.shape[0]))

    skcat = sub_keys.transpose(0, 2, 1, 3).reshape(PEER_HEADS, PEER_KEYS, 2 * PEER_HALF).astype(BF16)
    x2, eidx, gate = _cross(qca, kca, vca, x1, w_co.astype(BF16), row(norm_ffn_g), w_pq.astype(BF16), skcat,
                        min(512, seq), seq, mem_len)
    return _peer(eidx, gate, x2, row(norm_ffn_g), row(final_g), _pack_experts(peer_u, peer_v), min(128, n))


def kernel(x, mem, norm_mix_g, w_in, conv_w, b_igate, b_fgate, ml_norm_g, lambda_q1, lambda_k1, lambda_q2, lambda_k2, da_norm_g, w_out, norm_ca_g, norm_mem_g, w_cq, w_ck, w_cv, w_co, norm_ffn_g, w_pq, sub_keys, peer_u, peer_v, final_norm_g):
    batch, seq, d = x.shape
    mem_len = mem.shape[1]
    depth = w_in.shape[0]
    assert depth == 1, "final norm is fused into the last layer's PEER kernel; one layer supported"
    x2d = x.reshape(batch * seq, d)
    mem2d = mem.reshape(batch * mem_len, d)
    out = _layer(x2d, mem2d, batch, seq, mem_len, 0, norm_mix_g[0], w_in[0], conv_w[0], b_igate[0],
                 b_fgate[0], ml_norm_g[0], lambda_q1[0], lambda_k1[0], lambda_q2[0], lambda_k2[0],
                 da_norm_g[0], w_out[0], norm_ca_g[0], norm_mem_g[0], w_cq[0], w_ck[0], w_cv[0],
                 w_co[0], norm_ffn_g[0], w_pq[0], sub_keys[0], peer_u[0], peer_v[0], final_norm_g)
    return out.reshape(batch, seq, d)
```

```python
import functools
import math

import jax
import jax.numpy as jnp
import numpy as np
from jax import lax
from jax.experimental import pallas as pl
from jax.experimental.pallas import tpu as pltpu

F32 = jnp.float32
BF16 = jnp.bfloat16
EPS = 1e-6

ML_HEADS = 4
ML_HEAD_DIM = 128
ML_CHUNK = 64
CONV_WIDTH = 4
DA_HEADS = 4
DA_QK_DIM = 64
DA_VDIM = 128
CA_HEADS = 4
PEER_HEADS = 8
PEER_KEYS = 128
PEER_TOPK = 16
PEER_HALF = 64
LANES = 128

VMEM_LIMIT = 56 * 1024 * 1024


def _mm(a, b):
    return jnp.dot(a, b, preferred_element_type=F32)


def _mm_nt(a, b):
    return lax.dot_general(a, b, (((1,), (1,)), ((), ())), preferred_element_type=F32)


def _rms(x, g):
    return x * lax.rsqrt(jnp.mean(x * x, axis=-1, keepdims=True) + EPS) * g


def _params(sem):
    return pltpu.CompilerParams(dimension_semantics=sem, vmem_limit_bytes=VMEM_LIMIT)


def _inproj_kernel(x_ref, g_ref, w_ref, wg_ref, z_ref, gate_ref):
    hb = _rms(x_ref[...], g_ref[...]).astype(BF16)
    z_ref[...] = _mm(hb, w_ref[...]).astype(BF16)
    gate_ref[...] = _mm(hb, wg_ref[...])


def _inproj(x2d, g, w_main, w_gate, tm):
    n, d = x2d.shape
    wz = w_main.shape[1]
    return pl.pallas_call(
        _inproj_kernel,
        grid=(n // tm,),
        in_specs=[
            pl.BlockSpec((tm, d), lambda i: (i, 0)),
            pl.BlockSpec((1, d), lambda i: (0, 0)),
            pl.BlockSpec((d, wz), lambda i: (0, 0)),
            pl.BlockSpec((d, LANES), lambda i: (0, 0)),
        ],
        out_specs=[
            pl.BlockSpec((tm, wz), lambda i: (i, 0)),
            pl.BlockSpec((tm, LANES), lambda i: (i, 0)),
        ],
        out_shape=[
            jax.ShapeDtypeStruct((n, wz), BF16),
            jax.ShapeDtypeStruct((n, LANES), F32),
        ],
        compiler_params=_params(("parallel",)),
        name="inproj",
    )(x2d, g, w_main, w_gate)


def _mlstm_kernel(zqk_ref, zv_ref, zo_ref, gates_ref, convw_ref, gbias_ref, gml_ref, out_ref,
                  qk_scr, c_scr, *, seq, nb):
    width = ML_HEADS * ML_HEAD_DIM
    conv_rows = 128
    halo = 16

    w = convw_ref[...]
    col = lax.broadcasted_iota(jnp.int32, (1, 2 * width), 1)
    kscale = jnp.where(col >= width, ML_HEAD_DIM ** -0.5, 1.0).astype(F32)

    def conv_body(rb, _):
        r0 = pl.multiple_of(rb * conv_rows, conv_rows)
        main = zqk_ref[pl.ds(r0, conv_rows), :].astype(F32)
        prev0 = pl.multiple_of(jnp.maximum(r0 - halo, 0), halo)
        prev = zqk_ref[pl.ds(prev0, halo), :].astype(F32)
        prev = jnp.where(rb % (seq // conv_rows) > 0, prev, 0.0)
        xx = jnp.concatenate([prev, main], axis=0)
        acc = xx[halo:] * w[CONV_WIDTH - 1:CONV_WIDTH, :]
        for j in range(CONV_WIDTH - 1):
            sh = CONV_WIDTH - 1 - j
            acc = acc + pltpu.roll(xx, sh, 0)[halo:] * w[j:j + 1, :]
        y = acc * jax.nn.sigmoid(acc) * kscale
        qk_scr[pl.ds(r0, conv_rows), :] = y.astype(BF16)
        return 0

    lax.fori_loop(0, nb * seq // conv_rows, conv_body, 0)

    c_scr[...] = jnp.zeros_like(c_scr)
    L = ML_CHUNK
    row = lax.broadcasted_iota(jnp.int32, (L, L), 0)
    colL = lax.broadcasted_iota(jnp.int32, (L, L), 1)
    tril = (colL <= row)
    tril_b = tril.astype(BF16)
    lane128 = lax.broadcasted_iota(jnp.int32, (L, LANES), 1)
    ones_col = jnp.where(lane128 == 0, 1.0, 0.0).astype(BF16)
    gbias = gbias_ref[...]
    gml = gml_ref[...]

    def chunk_body(c, m_states):
        new_states = []
        for bb in range(nb):
            r0 = pl.multiple_of(bb * seq + c * L, L)
            gb = gates_ref[pl.ds(r0, L), :] + gbias
            lf = jnp.minimum(gb, 0.0) - jnp.log1p(jnp.exp(-jnp.abs(gb)))
            p = jnp.where(lane128 < ML_HEADS, gb, lf)
            p_hi = p.astype(BF16)
            r1 = p - p_hi.astype(F32)
            p_mid = r1.astype(BF16)
            p_lo = (r1 - p_mid.astype(F32)).astype(BF16)
            cum = _mm(tril_b, p_hi) + _mm(tril_b, p_mid) + _mm(tril_b, p_lo)
            q_mix = jnp.where(lane128 < ML_HEADS, p, cum)
            qt = q_mix.T
            for h in range(ML_HEADS):
                m_st = m_states[bb * ML_HEADS + h]
                ic_col = p[:, h:h + 1]
                b_col = cum[:, ML_HEADS + h:ML_HEADS + h + 1]
                ic_row = qt[h:h + 1, :]
                b_row = qt[ML_HEADS + h:ML_HEADS + h + 1, :]
                b_last = b_col[L - 1:L, :]
                log_d = jnp.where(tril, b_col - b_row + ic_row, -jnp.inf)
                m_inter = b_col + m_st
                m_t = jnp.maximum(m_inter, jnp.max(log_d, axis=-1, keepdims=True))
                dmat = jnp.exp(log_d - m_t)
                hs = slice(h * ML_HEAD_DIM, (h + 1) * ML_HEAD_DIM)
                ks = slice(width + h * ML_HEAD_DIM, width + (h + 1) * ML_HEAD_DIM)
                qc = qk_scr[pl.ds(r0, L), hs]
                kc = qk_scr[pl.ds(r0, L), ks]
                vc = zv_ref[pl.ds(r0, L), hs]
                v_aug = jnp.concatenate([vc, ones_col], axis=1)
                sc = _mm_nt(qc, kc) * dmat
                inter = jnp.exp(m_inter - m_t)
                c_old = c_scr[bb * ML_HEADS + h]
                num_aug = _mm(sc.astype(BF16), v_aug) + inter * _mm(qc, c_old.astype(BF16))
                num = num_aug[:, :ML_HEAD_DIM]
                den = num_aug[:, ML_HEAD_DIM:ML_HEAD_DIM + 1]
                hraw = num / jnp.maximum(jnp.abs(den), jnp.exp(-m_t))
                hn = _rms(hraw, gml[:, hs])
                og = zo_ref[pl.ds(r0, L), hs].astype(F32)
                out_ref[pl.ds(r0, L), hs] = (hn * jax.nn.sigmoid(og)).astype(BF16)
                g_col = b_last - b_col + ic_col
                m_next = jnp.maximum(b_last + m_st, jnp.max(g_col, axis=0, keepdims=True))
                decay = jnp.exp(b_last + m_st - m_next)
                wgt = jnp.exp(g_col - m_next)
                kw_t = (wgt * kc.astype(F32)).T.astype(BF16)
                c_scr[bb * ML_HEADS + h] = decay * c_old + _mm(kw_t, v_aug)
                new_states.append(m_next)
        return tuple(new_states)

    init = tuple(jnp.zeros((1, 1), F32) for _ in range(nb * ML_HEADS))
    lax.fori_loop(0, seq // L, chunk_body, init)


def _mlstm(z, gates, conv_w, gbias, gml, batch, seq):
    n = batch * seq
    width = ML_HEADS * ML_HEAD_DIM
    nb = 2 if batch % 2 == 0 else 1
    rows = nb * seq
    return pl.pallas_call(
        functools.partial(_mlstm_kernel, seq=seq, nb=nb),
        grid=(batch // nb,),
        in_specs=[
            pl.BlockSpec((rows, 2 * width), lambda b: (b, 0)),
            pl.BlockSpec((rows, width), lambda b: (b, 2)),
            pl.BlockSpec((rows, width), lambda b: (b, 3)),
            pl.BlockSpec((rows, LANES), lambda b: (b, 0)),
            pl.BlockSpec((CONV_WIDTH, 2 * width), lambda b: (0, 0)),
            pl.BlockSpec((1, LANES), lambda b: (0, 0)),
            pl.BlockSpec((1, width), lambda b: (0, 0)),
        ],
        out_specs=pl.BlockSpec((rows, width), lambda b: (b, 0)),
        out_shape=jax.ShapeDtypeStruct((n, width), BF16),
        scratch_shapes=[
            pltpu.VMEM((rows, 2 * width), BF16),
            pltpu.VMEM((nb * ML_HEADS, ML_HEAD_DIM, 2 * ML_HEAD_DIM), F32),
        ],
        compiler_params=_params(("parallel",)),
        name="mlstm",
    )(z, z, z, gates, conv_w, gbias, gml)


def _diffattn_kernel(slopes_ref, q_ref, k_ref, v_ref, lq1_ref, lk1_ref, lq2_ref, lk2_ref, g_ref,
                     o_ref, vt_ref, bias_ref, acc_ref, m_ref, l_ref, *, tq, lam_init):
    tk = tq
    h = pl.program_id(1)
    i = pl.program_id(2)
    slope = slopes_ref[h]
    scale = DA_QK_DIM ** -0.5

    @pl.when(i == 0)
    def _():
        vt_ref[...] = v_ref[...].astype(F32).T.astype(BF16)
        kk = lax.broadcasted_iota(jnp.int32, (tk, 2 * tq), 0)
        qq = lax.broadcasted_iota(jnp.int32, (tk, 2 * tq), 1) & (tq - 1)
        kmq = kk - qq
        rel = kmq.astype(F32) * (-slope)
        bias_ref[0] = rel
        bias_ref[1] = jnp.where(kmq <= 0, rel, jnp.inf)

    q = (q_ref[...].astype(F32) * scale).astype(BF16)
    lane = lax.broadcasted_iota(jnp.int32, q.shape, 1)
    zero = jnp.zeros_like(q)
    q2 = jnp.concatenate([jnp.where(lane < DA_QK_DIM, q, zero), jnp.where(lane >= DA_QK_DIM, q, zero)], axis=0)
    m_ref[...] = jnp.full(m_ref.shape, -jnp.inf, F32)
    l_ref[...] = jnp.zeros(l_ref.shape, F32)
    acc_ref[...] = jnp.zeros(acc_ref.shape, F32)

    def block(j, masked):
        k0 = pl.multiple_of(j * tk, tk)
        ks = k_ref[pl.ds(k0, tk), :]
        off = ((i - j) * tq).astype(F32) * slope
        s = _mm_nt(ks, q2) - bias_ref[1 if masked else 0]
        m_old = m_ref[...]
        m_new = jnp.maximum(m_old, jnp.max(s, axis=0, keepdims=True) - off)
        alpha = jnp.exp(m_old - m_new)
        p = jnp.exp(s - (m_new + off))
        l_ref[...] = alpha * l_ref[...] + jnp.sum(p, axis=0, keepdims=True)
        acc_ref[...] = alpha * acc_ref[...] + _mm(vt_ref[:, pl.ds(k0, tk)], p.astype(BF16))
        m_ref[...] = m_new

    def full_body(j, carry):
        block(j, False)
        return carry

    lax.fori_loop(0, i, full_body, 0)
    block(i, True)

    lam = (jnp.exp(jnp.sum(lq1_ref[...] * lk1_ref[...], axis=-1, keepdims=True))
           - jnp.exp(jnp.sum(lq2_ref[...] * lk2_ref[...], axis=-1, keepdims=True)) + lam_init)
    ot = acc_ref[:, :tq] / l_ref[:, :tq] - lam * (acc_ref[:, tq:] / l_ref[:, tq:])
    o_ref[...] = (_rms(ot.T, g_ref[...]) * (1.0 - lam_init)).astype(BF16)


def _diffattn(z, slopes, lq1, lk1, lq2, lk2, gda, batch, seq, tq, lam_init):
    n = batch * seq
    nq = seq // tq
    qcol0 = (4 * ML_HEADS * ML_HEAD_DIM) // LANES
    kcol0 = qcol0 + (DA_HEADS * DA_VDIM) // LANES
    vcol0 = kcol0 + (DA_HEADS * DA_VDIM) // LANES
    lam_spec = pl.BlockSpec((1, DA_QK_DIM), lambda b, h, i: (0, 0))
    return pl.pallas_call(
        functools.partial(_diffattn_kernel, tq=tq, lam_init=lam_init),
        grid=(batch, DA_HEADS, nq),
        in_specs=[
            pl.BlockSpec(memory_space=pltpu.SMEM),
            pl.BlockSpec((tq, LANES), lambda b, h, i: (b * nq + i, qcol0 + h)),
            pl.BlockSpec((seq, LANES), lambda b, h, i: (b, kcol0 + h)),
            pl.BlockSpec((seq, LANES), lambda b, h, i: (b, vcol0 + h)),
            lam_spec, lam_spec, lam_spec, lam_spec,
            pl.BlockSpec((1, DA_VDIM), lambda b, h, i: (0, h)),
        ],
        out_specs=pl.BlockSpec((tq, DA_VDIM), lambda b, h, i: (b * nq + i, h)),
        out_shape=jax.ShapeDtypeStruct((n, DA_HEADS * DA_VDIM), BF16),
        scratch_shapes=[
            pltpu.VMEM((DA_VDIM, seq), BF16),
            pltpu.VMEM((2, tq, 2 * tq), F32),
            pltpu.VMEM((DA_VDIM, 2 * tq), F32),
            pltpu.VMEM((1, 2 * tq), F32),
            pltpu.VMEM((1, 2 * tq), F32),
        ],
        compiler_params=_params(("parallel", "parallel", "arbitrary")),
        name="diffattn",
    )(slopes, z, z, z, lq1, lk1, lq2, lk2, gda)


def _outproj_kernel(x_ref, hml_ref, hda_ref, wo1_ref, wo2_ref, g_ref, wq_ref, x1_ref, q_ref):
    x1 = x_ref[...] + _mm(hml_ref[...], wo1_ref[...]) + _mm(hda_ref[...], wo2_ref[...])
    x1_ref[...] = x1
    q_ref[...] = _mm(_rms(x1, g_ref[...]).astype(BF16), wq_ref[...]).astype(BF16)


def _outproj(x2d, hml, hda, wo1, wo2, g, wq, tm):
    n, d = x2d.shape
    half = hml.shape[1]
    row = lambda i: (i, 0)
    fixed = lambda i: (0, 0)
    return pl.pallas_call(
        _outproj_kernel,
        grid=(n // tm,),
        in_specs=[
            pl.BlockSpec((tm, d), row),
            pl.BlockSpec((tm, half), row),
            pl.BlockSpec((tm, half), row),
            pl.BlockSpec((half, d), fixed),
            pl.BlockSpec((half, d), fixed),
            pl.BlockSpec((1, d), fixed),
            pl.BlockSpec((d, d), fixed),
        ],
        out_specs=[pl.BlockSpec((tm, d), row), pl.BlockSpec((tm, d), row)],
        out_shape=[jax.ShapeDtypeStruct((n, d), F32), jax.ShapeDtypeStruct((n, d), BF16)],
        compiler_params=_params(("parallel",)),
        name="outproj",
    )(x2d, hml, hda, wo1, wo2, g, wq)


def _memkv_kernel(m_ref, g_ref, wk_ref, wv_ref, k_ref, v_ref):
    mb = _rms(m_ref[...], g_ref[...]).astype(BF16)
    k_ref[...] = _mm(mb, wk_ref[...]).astype(BF16)
    v_ref[...] = _mm(mb, wv_ref[...]).astype(BF16)


def _memkv(mem2d, g, wk, wv, tm):
    n, d = mem2d.shape
    row = lambda i: (i, 0)
    fixed = lambda i: (0, 0)
    return pl.pallas_call(
        _memkv_kernel,
        grid=(n // tm,),
        in_specs=[pl.BlockSpec((tm, d), row), pl.BlockSpec((1, d), fixed),
                  pl.BlockSpec((d, d), fixed), pl.BlockSpec((d, d), fixed)],
        out_specs=[pl.BlockSpec((tm, d), row), pl.BlockSpec((tm, d), row)],
        out_shape=[jax.ShapeDtypeStruct((n, d), BF16), jax.ShapeDtypeStruct((n, d), BF16)],
        compiler_params=_params(("parallel",)),
        name="memkv",
    )(mem2d, g, wk, wv)


def _cross_kernel(q_ref, k_ref, v_ref, x1_ref, wco_ref, g_ref, wpq_ref, sk_ref,
                  x2_ref, eidx_ref, gate_ref):
    d = q_ref.shape[1]
    hd = d // CA_HEADS
    scale = hd ** -0.5
    outs = []
    for hh in range(CA_HEADS):
        cs = slice(hh * hd, (hh + 1) * hd)
        s = _mm_nt(q_ref[:, cs], k_ref[:, cs]) * scale
        s = s - jnp.max(s, axis=-1, keepdims=True)
        e = jnp.exp(s)
        p = e / jnp.sum(e, axis=-1, keepdims=True)
        outs.append(_mm(p.astype(BF16), v_ref[:, cs]).astype(BF16))
    o = jnp.concatenate(outs, axis=1)
    x2 = x1_ref[...] + _mm(o, wco_ref[...])
    x2_ref[...] = x2
    xn = _rms(x2, g_ref[...])
    qp = _mm(xn.astype(BF16), wpq_ref[...]).astype(BF16)
    lane = lax.broadcasted_iota(jnp.int32, (qp.shape[0], LANES), 1)
    gates = []
    eids = []
    for hp in range(PEER_HEADS):
        qh = qp[:, hp * LANES:(hp + 1) * LANES]
        zero = jnp.zeros_like(qh)
        sk = sk_ref[hp]
        g, e = _route_head(_mm_nt(sk, jnp.where(lane < PEER_HALF, qh, zero)),
                           _mm_nt(sk, jnp.where(lane >= PEER_HALF, qh, zero)))
        gates.append(g)
        eids.append(e)
    gate_ref[...] = jnp.concatenate(gates, axis=0).T
    eidx_ref[...] = jnp.concatenate(eids, axis=0).T


def _cross(qca, kca, vca, x1, wco, g, wpq, skcat, tm, seq, mem_len):
    n, d = x1.shape
    per_b = seq // tm
    row = lambda i: (i, 0)
    fixed = lambda i: (0, 0)
    memb = lambda i: (i // per_b, 0)
    width = PEER_HEADS * PEER_TOPK
    return pl.pallas_call(
        _cross_kernel,
        grid=(n // tm,),
        in_specs=[
            pl.BlockSpec((tm, d), row),
            pl.BlockSpec((mem_len, d), memb),
            pl.BlockSpec((mem_len, d), memb),
            pl.BlockSpec((tm, d), row),
            pl.BlockSpec((d, d), fixed),
            pl.BlockSpec((1, d), fixed),
            pl.BlockSpec((d, PEER_HEADS * LANES), fixed),
            pl.BlockSpec((PEER_HEADS, PEER_KEYS, LANES), lambda i: (0, 0, 0)),
        ],
        out_specs=[
            pl.BlockSpec((tm, d), row),
            pl.BlockSpec((tm, width), row),
            pl.BlockSpec((tm, width), row),
        ],
        out_shape=[
            jax.ShapeDtypeStruct((n, d), F32),
            jax.ShapeDtypeStruct((n, width), jnp.int32),
            jax.ShapeDtypeStruct((n, width), F32),
        ],
        compiler_params=_params(("parallel",)),
        name="cross",
    )(qca, kca, vca, x1, wco, g, wpq, skcat)


def _extract_topk(x, payload):
    rows, t = x.shape
    rid = lax.broadcasted_iota(jnp.int32, (rows, t), 0)
    krow = lax.broadcasted_iota(jnp.int32, (PEER_TOPK, t), 0)
    vals = jnp.zeros((PEER_TOPK, t), F32)
    picked = jnp.zeros((PEER_TOPK, t), jnp.int32)
    for r in range(PEER_TOPK):
        m = jnp.max(x, axis=0, keepdims=True)
        idx = jnp.min(jnp.where(x == m, rid, rows), axis=0, keepdims=True)
        hit = rid == idx
        if payload is None:
            sel = idx
        else:
            sel = jnp.sum(jnp.where(hit, payload, 0), axis=0, keepdims=True)
        vals = jnp.where(krow == r, m, vals)
        picked = jnp.where(krow == r, sel, picked)
        x = jnp.where(hit, -jnp.inf, x)
    return vals, picked


def _pair_candidates(va, ia, vb, ib):
    t = va.shape[1]
    sub = 8
    row = lax.broadcasted_iota(jnp.int32, (sub, t), 0)
    vals = [va[0:1] + vb]
    ids = [ia[0:1] * PEER_KEYS + ib]
    for i in range(1, sub):
        keep = PEER_TOPK // (i + 1)
        v = va[i:i + 1] + vb[:sub]
        vals.append(v if keep >= sub else jnp.where(row < keep, v, -jnp.inf))
        ids.append(ia[i:i + 1] * PEER_KEYS + ib[:sub])
    vals.append(va[sub:] + vb[0:1])
    ids.append(ia[sub:] * PEER_KEYS + ib[0:1])
    return jnp.concatenate(vals, axis=0), jnp.concatenate(ids, axis=0)


def _route_head(s0, s1):
    va, ia = _extract_topk(s0, None)
    vb, ib = _extract_topk(s1, None)
    cand, cid = _pair_candidates(va, ia, vb, ib)
    bv, be = _extract_topk(cand, cid)
    e = jnp.exp(bv - bv[0:1, :])
    return e / jnp.sum(e, axis=0, keepdims=True), be


PEER_SLOTS = 8
PEER_ROW_TILES = 16
PEER_PITCH = PEER_ROW_TILES + 1


def _peer_kernel(eidx_ref, gate_ref, x2_ref, gffn_ref, gfin_ref, uv_hbm, out_ref, buf, sem, acc_ref, xn_ref,
                 *, tb):
    nsel = PEER_HEADS * PEER_TOPK
    d = x2_ref.shape[1]
    xn_ref[...] = _rms(x2_ref[...], gffn_ref[...])
    i = pl.program_id(0)
    last_step = pl.num_programs(0) - 1
    ngroups = tb // PEER_SLOTS

    sub = PEER_ROW_TILES

    def issue(idx_ref, t, slot):
        for k in range(nsel):
            e8 = pl.multiple_of(idx_ref[t, k] * sub, sub)
            dst = buf.at[pl.ds((slot * nsel + k) * PEER_PITCH, sub), :]
            pltpu.make_async_copy(uv_hbm.at[pl.ds(e8, sub), :], dst, sem.at[slot]).start(priority=k % 2)

    def wait_slot(slot):
        whole = buf.at[pl.ds(0, nsel * sub), :]
        pltpu.make_async_copy(whole, whole, sem.at[slot]).wait()

    eye = (lax.broadcasted_iota(jnp.int32, (nsel, nsel), 0) == lax.broadcasted_iota(jnp.int32, (nsel, nsel), 1))

    tiles = d // LANES
    per_vreg = 8

    def load_rows(s, half):
        groups = []
        for g in range(nsel // per_vreg):
            base = (s * nsel + g * per_vreg) * PEER_PITCH + half * tiles
            groups.append(jnp.concatenate(
                [buf[pl.ds(base + c, per_vreg, stride=PEER_PITCH), :] for c in range(tiles)], axis=1))
        return jnp.concatenate(groups, axis=0)

    def compute(t, s):
        x = xn_ref[pl.ds(t, 1), :]
        u = load_rows(s, 0)
        act = jnp.sum(u * x, axis=-1, keepdims=True)
        g_row = gate_ref[pl.ds(t, 1), :]
        g_col = jnp.sum(jnp.where(eye, g_row, 0.0), axis=-1, keepdims=True)
        coef = 0.5 * act * (1.0 + lax.erf(act * (2.0 ** -0.5))) * g_col
        v = load_rows(s, 1)
        return jnp.sum(coef * v, axis=0, keepdims=True)

    @pl.when(i == 0)
    def _():
        for s in range(PEER_SLOTS - 1):
            issue(eidx_ref, s, s)

    def group(g, carry):
        for s in range(PEER_SLOTS):
            t = g * PEER_SLOTS + s
            wait_slot(s)
            row = compute(t, s)
            issue(eidx_ref, t + PEER_SLOTS - 1, (s - 1) % PEER_SLOTS)
            acc_ref[pl.ds(t, 1), :] = row
        return carry

    lax.fori_loop(0, ngroups, group, 0)

    @pl.when(i == last_step)
    def _():
        for s in range(PEER_SLOTS - 1):
            wait_slot(s)

    out_ref[...] = _rms(x2_ref[...] + acc_ref[...], gfin_ref[...])


def _pack_experts(peer_u, peer_v):
    e, d = peer_u.shape
    both = jnp.concatenate([peer_u.reshape(e, d // LANES, LANES), peer_v.reshape(e, d // LANES, LANES)], axis=1)
    return both.reshape(e * PEER_ROW_TILES, LANES)


def _peer(eidx, gate, x2, gffn, gfin, uv, tb):
    n, d = x2.shape
    nsel = PEER_HEADS * PEER_TOPK
    nsteps = n // tb
    row = lambda i: (i, 0)
    blocks = eidx.reshape(nsteps, tb, nsel)
    heads = jnp.concatenate([blocks[1:, :PEER_SLOTS], jnp.zeros((1, PEER_SLOTS, nsel), eidx.dtype)], axis=0)
    eidx_ext = jnp.concatenate([blocks, heads], axis=1)
    return pl.pallas_call(
        functools.partial(_peer_kernel, tb=tb),
        grid=(nsteps,),
        in_specs=[
            pl.BlockSpec((None, tb + PEER_SLOTS, nsel), lambda i: (i, 0, 0), memory_space=pltpu.SMEM),
            pl.BlockSpec((tb, nsel), row),
            pl.BlockSpec((tb, d), row),
            pl.BlockSpec((1, d), lambda i: (0, 0)),
            pl.BlockSpec((1, d), lambda i: (0, 0)),
            pl.BlockSpec(memory_space=pl.ANY),
        ],
        out_specs=pl.BlockSpec((tb, d), row),
        out_shape=jax.ShapeDtypeStruct((n, d), F32),
        scratch_shapes=[
            pltpu.VMEM((PEER_SLOTS * nsel * PEER_PITCH, LANES), F32),
            pltpu.SemaphoreType.DMA((PEER_SLOTS,)),
            pltpu.VMEM((tb, d), F32),
            pltpu.VMEM((tb, d), F32),
        ],
        compiler_params=_params(("arbitrary",)),
        name="peer",
    )(eidx_ext, gate, x2, gffn, gfin, uv)


def _layer(x2d, mem2d, batch, seq, mem_len, layer_idx, norm_mix_g, w_in, conv_w, b_igate, b_fgate,
           ml_norm_g, lambda_q1, lambda_k1, lambda_q2, lambda_k2, da_norm_g, w_out, norm_ca_g,
           norm_mem_g, w_cq, w_ck, w_cv, w_co, norm_ffn_g, w_pq, sub_keys, peer_u, peer_v, final_g):
    n, d = x2d.shape
    mlw = ML_HEADS * ML_HEAD_DIM
    gate0 = 4 * mlw
    gate1 = gate0 + 2 * ML_HEADS
    tm = min(512, n)

    w_main = jnp.concatenate([w_in[:, :gate0], w_in[:, gate1:]], axis=1).astype(BF16)
    w_gate = jnp.pad(w_in[:, gate0:gate1], ((0, 0), (0, LANES - 2 * ML_HEADS))).astype(BF16)
    gbias = jnp.pad(jnp.concatenate([b_igate, b_fgate]), (0, LANES - 2 * ML_HEADS)).reshape(1, LANES)
    row = lambda a: a.reshape(1, -1)

    z, gates = _inproj(x2d, row(norm_mix_g), w_main, w_gate, tm)
    hml = _mlstm(z, gates, conv_w, gbias, row(ml_norm_g), batch, seq)

    lam_init = 0.8 - 0.6 * math.exp(-0.3 * layer_idx)
    slopes = jnp.asarray(np.array([2.0 ** (-8.0 * (i + 1) / DA_HEADS) for i in range(DA_HEADS)], np.float32))
    hda = _diffattn(z, slopes, row(lambda_q1), row(lambda_k1), row(lambda_q2), row(lambda_k2),
                    row(da_norm_g), batch, seq, min(512, seq), lam_init)

    wo = w_out.astype(BF16)
    x1, qca = _outproj(x2d, hml, hda, wo[:mlw], wo[mlw:], row(norm_ca_g), w_cq.astype(BF16), tm)
    kca, vca = _memkv(mem2d, row(norm_mem_g), w_ck.astype(BF16), w_cv.astype(BF16), min(512, mem2d.shape[0]))

    skcat = sub_keys.transpose(0, 2, 1, 3).reshape(PEER_HEADS, PEER_KEYS, 2 * PEER_HALF).astype(BF16)
    x2, eidx, gate = _cross(qca, kca, vca, x1, w_co.astype(BF16), row(norm_ffn_g), w_pq.astype(BF16), skcat,
                        min(512, seq), seq, mem_len)
    return _peer(eidx, gate, x2, row(norm_ffn_g), row(final_g), _pack_experts(peer_u, peer_v), min(128, n))


def kernel(x, mem, norm_mix_g, w_in, conv_w, b_igate, b_fgate, ml_norm_g, lambda_q1, lambda_k1, lambda_q2, lambda_k2, da_norm_g, w_out, norm_ca_g, norm_mem_g, w_cq, w_ck, w_cv, w_co, norm_ffn_g, w_pq, sub_keys, peer_u, peer_v, final_norm_g):
    batch, seq, d = x.shape
    mem_len = mem.shape[1]
    depth = w_in.shape[0]
    assert depth == 1, "final norm is fused into the last layer's PEER kernel; one layer supported"
    x2d = x.reshape(batch * seq, d)
    mem2d = mem.reshape(batch * mem_len, d)
    out = _layer(x2d, mem2d, batch, seq, mem_len, 0, norm_mix_g[0], w_in[0], conv_w[0], b_igate[0],
                 b_fgate[0], ml_norm_g[0], lambda_q1[0], lambda_k1[0], lambda_q2[0], lambda_k2[0],
                 da_norm_g[0], w_out[0], norm_ca_g[0], norm_mem_g[0], w_cq[0], w_ck[0], w_cv[0],
                 w_co[0], norm_ffn_g[0], w_pq[0], sub_keys[0], peer_u[0], peer_v[0], final_norm_g)
    return out.reshape(batch, seq, d)
```
